```python
import jax
import jax.numpy as jnp
from jax import lax
import numpy as np

D_MODEL = 4096
BATCH = 4
SEQ = 2048
DEPTH = 2

CTX_LEN = 256
GRID_W = 64
CONV_CH = 2048
CONV_K = 31
RET_HEADS = 8
RET_HEAD_DIM = 256
RET_WIDTH = RET_HEADS * RET_HEAD_DIM
RET_CHUNK = 128
N_EXPERTS = 64
N_GROUPS = 8
TOPK_GROUPS = 4
TOP_K = 8
EXPERT_HIDDEN = 256
SHARED_HIDDEN = 1024
ROUTED_SCALE = 2.5
N_MOD = 6
NORM_EPS = 1e-6
ROPE_BASE = 10000.0
PROJ_COLS = 2 * CONV_CH + 4 * RET_WIDTH + 2 * D_MODEL
SPLITS = (2 * CONV_CH, 2 * CONV_CH + RET_WIDTH, 2 * CONV_CH + 2 * RET_WIDTH,
          2 * CONV_CH + 3 * RET_WIDTH, 2 * CONV_CH + 4 * RET_WIDTH)

kernel_name = "hybrid_conformer_retention_moe_dit"


def rmsnorm(x, g):
    xf = x.astype(jnp.float32)
    y = xf * lax.rsqrt(jnp.mean(xf * xf, axis=-1, keepdims=True) + NORM_EPS)
    return (y * g.astype(jnp.float32)).astype(x.dtype)


def layernorm(x, g, b):
    xf = x.astype(jnp.float32)
    mu = jnp.mean(xf, axis=-1, keepdims=True)
    var = jnp.mean(jnp.square(xf - mu), axis=-1, keepdims=True)
    y = (xf - mu) * lax.rsqrt(var + NORM_EPS)
    return (y * g.astype(jnp.float32) + b.astype(jnp.float32)).astype(x.dtype)


def conv_module(u, conv_w, conv_b, ln_g, ln_b, w_proj):
    a, gate = jnp.split(u, 2, axis=-1)
    glu = a * jax.nn.sigmoid(gate)
    y = lax.conv_general_dilated(glu, conv_w[:, None, :].astype(glu.dtype), window_strides=(1,),
                                 padding='SAME', dimension_numbers=('NWC', 'WIO', 'NWC'),
                                 feature_group_count=CONV_CH) + conv_b
    y = jax.nn.silu(layernorm(y, ln_g, ln_b))
    return y @ w_proj


def heads(t):
    b, n, _ = t.shape
    return t.astype(jnp.float32).reshape(b, n, RET_HEADS, RET_HEAD_DIM)


def rope_2d(x):
    s = x.shape[1]
    rows = s // GRID_W
    row = jnp.repeat(jnp.arange(rows, dtype=jnp.float32), GRID_W)
    col = jnp.tile(jnp.arange(GRID_W, dtype=jnp.float32), rows)
    half = x.shape[-1] // 2
    quarter = half // 2
    inv_freq = ROPE_BASE ** (-jnp.arange(quarter, dtype=jnp.float32) / quarter)

    def rot(part, pos):
        ang = pos[:, None] * inv_freq[None, :]
        cos = jnp.cos(ang)[None, :, None, :]
        sin = jnp.sin(ang)[None, :, None, :]
        p1, p2 = part[..., :quarter], part[..., quarter:]
        return jnp.concatenate([p1 * cos - p2 * sin, p2 * cos + p1 * sin], axis=-1)

    return jnp.concatenate([rot(x[..., :half], row), rot(x[..., half:], col)], axis=-1)


def retention_scan(q, k, v, log_gamma, state0):
    b, t, h, dk = q.shape
    dv = v.shape[-1]
    n = t // RET_CHUNK
    lg = log_gamma.astype(jnp.float32)
    qc = q.reshape(b, n, RET_CHUNK, h, dk)
    kc = k.reshape(b, n, RET_CHUNK, h, dk)
    vc = v.reshape(b, n, RET_CHUNK, h, dv)
    pos = jnp.arange(RET_CHUNK, dtype=jnp.float32)
    rel = pos[:, None] - pos[None, :]
    decay = jnp.where(rel >= 0, jnp.exp(lg[:, None, None] * jnp.maximum(rel, 0.0)[None]), 0.0)
    scores = jnp.einsum('bnihd,bnjhd->bnhij', qc, kc) * decay
    intra = jnp.einsum('bnhij,bnjhe->bnihe', scores, vc)
    zeta = jnp.exp(lg[:, None] * (RET_CHUNK - 1 - pos)[None, :])
    xi = jnp.exp(lg[:, None] * (pos + 1.0)[None, :])
    chunk_decay = jnp.exp(lg * RET_CHUNK)[None, :, None, None]
    kv = jnp.einsum('bnjhd,bnjhe,hj->bnhde', kc, vc, zeta)

    def step(state, inp):
        q_n, kv_n = inp
        cross = jnp.einsum('bihd,bhde,hi->bihe', q_n, state, xi)
        return state * chunk_decay + kv_n, cross

    final, cross = lax.scan(step, state0, (jnp.swapaxes(qc, 0, 1), jnp.swapaxes(kv, 0, 1)))
    out = intra + jnp.swapaxes(cross, 0, 1)
    return out.reshape(b, t, h, dv), final


def retention_final_state(k, v, log_gamma):
    t = k.shape[1]
    w = jnp.exp(log_gamma.astype(jnp.float32)[None, :] * (t - 1 - jnp.arange(t, dtype=jnp.float32))[:, None])
    return jnp.einsum('bthd,bthe,th->bhde', k, v, w)


def ret_readout(o, g, w_proj, dtype):
    b, t = o.shape[:2]
    o = o * lax.rsqrt(jnp.mean(o * o, axis=-1, keepdims=True) + NORM_EPS)
    y = o.reshape(b, t, RET_WIDTH) * jax.nn.silu(g.astype(jnp.float32))
    return y.astype(dtype) @ w_proj


def merge(y_conv, y_ret, gates, w_out):
    g_conv, g_ret = jnp.split(gates, 2, axis=-1)
    return (jax.nn.sigmoid(g_conv) * y_conv + jax.nn.sigmoid(g_ret) * y_ret) @ w_out


def mixer(hl, hc, w_in, conv_w, conv_b, ln_g, ln_b, w_conv_out, lg_f, lg_b, w_ret_out, w_out, ctx_out):
    b = hl.shape[0]
    dt = hl.dtype
    scale = RET_HEAD_DIM ** -0.5
    zero = jnp.zeros((b, RET_HEADS, RET_HEAD_DIM, RET_HEAD_DIM), jnp.float32)
    u_l, q_l, k_l, v_l, g_l, gt_l = jnp.split(hl @ w_in, SPLITS, axis=-1)
    conv_l = conv_module(u_l, conv_w, conv_b, ln_g, ln_b, w_conv_out)
    if ctx_out:
        u_c, q_c, k_c, v_c, g_c, gt_c = jnp.split(hc @ w_in, SPLITS, axis=-1)
    else:
        k_c, v_c = jnp.split(hc @ w_in[:, SPLITS[1]:SPLITS[3]], 2, axis=-1)
    kc, vc = heads(k_c), heads(v_c)
    if ctx_out:
        qc = heads(q_c) * scale
        oc_f, st_f = retention_scan(qc, kc, vc, lg_f, zero)
        oc_b, st_b = retention_scan(jnp.flip(qc, 1), jnp.flip(kc, 1), jnp.flip(vc, 1), lg_b, zero)
        ret_c = ret_readout(oc_f + jnp.flip(oc_b, 1), g_c, w_ret_out, dt)
        conv_c = conv_module(u_c, conv_w, conv_b, ln_g, ln_b, w_conv_out)
        ctx_mix = merge(conv_c, ret_c, gt_c, w_out)
    else:
        st_f = retention_final_state(kc, vc, lg_f)
        st_b = retention_final_state(jnp.flip(kc, 1), jnp.flip(vc, 1), lg_b)
        ctx_mix = None
    ql = rope_2d(heads(q_l)) * scale
    kl = rope_2d(heads(k_l))
    vl = heads(v_l)
    ol_f, _ = retention_scan(ql, kl, vl, lg_f, st_f)
    ol_b, _ = retention_scan(jnp.flip(ql, 1), jnp.flip(kl, 1), jnp.flip(vl, 1), lg_b, st_b)
    ret_l = ret_readout(ol_f + jnp.flip(ol_b, 1), g_l, w_ret_out, dt)
    lat_mix = merge(conv_l, ret_l, gt_l, w_out)
    return lat_mix, ctx_mix


def moe(h, router_w, router_bias, w1, w3, w2, s1, s3, s2):
    t = h.shape[0]
    scores = jax.nn.sigmoid((h @ router_w).astype(jnp.float32))
    biased = scores + router_bias.astype(jnp.float32)
    grp = biased.reshape(t, N_GROUPS, N_EXPERTS // N_GROUPS)
    grp_score = jnp.sum(lax.top_k(grp, 2)[0], axis=-1)
    _, grp_idx = lax.top_k(grp_score, TOPK_GROUPS)
    grp_mask = jnp.sum(jax.nn.one_hot(grp_idx, N_GROUPS, dtype=jnp.float32), axis=-2)
    mask = jnp.repeat(grp_mask, N_EXPERTS // N_GROUPS, axis=-1) > 0
    _, idx = lax.top_k(jnp.where(mask, biased, -jnp.inf), TOP_K)
    sel = jnp.take_along_axis(scores, idx, axis=-1)
    wts = sel / jnp.sum(sel, axis=-1, keepdims=True) * ROUTED_SCALE
    comb = jnp.sum(jax.nn.one_hot(idx, N_EXPERTS, dtype=jnp.float32) * wts[..., None], axis=-2)

    def expert_step(acc, ew):
        e1, e3, e2, ce = ew
        a = jax.nn.silu(h @ e1) * (h @ e3)
        return acc + (a * ce[:, None].astype(h.dtype)) @ e2, None

    routed, _ = lax.scan(expert_step, jnp.zeros_like(h), (w1, w3, w2, comb.T))
    shared = (jax.nn.silu(h @ s1) * (h @ s3)) @ s2
    return routed + shared


def setup_inputs(seed: int = 0) -> dict:
    key = jax.random.key(seed)
    ks = jax.random.split(key, 27)
    f32 = jnp.float32
    d = D_MODEL

    def nrm(k, shape, scale):
        return jax.random.normal(k, shape, f32) * scale

    base_lg = jnp.log1p(-jnp.exp2(-5.0 - jnp.arange(RET_HEADS, dtype=f32)))
    return {
        "x": nrm(ks[0], (BATCH, SEQ, d), 1.0),
        "c": nrm(ks[1], (BATCH, d), 1.0),
        "ctx": nrm(ks[2], (BATCH, CTX_LEN, d), 1.0),
        "c_ctx": nrm(ks[3], (d,), 1.0),
        "mod_w": nrm(ks[4], (DEPTH, d, N_MOD * d), 0.5 * d ** -0.5),
        "mod_b": nrm(ks[5], (DEPTH, N_MOD * d), 0.02),
        "norm1_g": 1.0 + nrm(ks[6], (DEPTH, d), 0.02),
        "norm2_g": 1.0 + nrm(ks[7], (DEPTH, d), 0.02),
        "w_in": nrm(ks[8], (DEPTH, d, PROJ_COLS), d ** -0.5),
        "conv_w": nrm(ks[9], (DEPTH, CONV_K, CONV_CH), CONV_K ** -0.5),
        "conv_b": nrm(ks[10], (DEPTH, CONV_CH), 0.02),
        "conv_ln_g": 1.0 + nrm(ks[11], (DEPTH, CONV_CH), 0.02),
        "conv_ln_b": nrm(ks[12], (DEPTH, CONV_CH), 0.02),
        "w_conv_out": nrm(ks[13], (DEPTH, CONV_CH, d), CONV_CH ** -0.5),
        "ret_log_gamma_fwd": base_lg * (1.0 + nrm(ks[14], (DEPTH, RET_HEADS), 0.1)),
        "ret_log_gamma_bwd": base_lg * (1.0 + nrm(ks[15], (DEPTH, RET_HEADS), 0.1)),
        "w_ret_out": nrm(ks[16], (DEPTH, RET_WIDTH, d), RET_WIDTH ** -0.5),
        "w_merge_out": nrm(ks[17], (DEPTH, d, d), d ** -0.5),
        "router_w": nrm(ks[18], (DEPTH, d, N_EXPERTS), d ** -0.5),
        "router_bias": nrm(ks[19], (DEPTH, N_EXPERTS), 0.01),
        "exp_w1": nrm(ks[20], (DEPTH, N_EXPERTS, d, EXPERT_HIDDEN), d ** -0.5),
        "exp_w3": nrm(ks[21], (DEPTH, N_EXPERTS, d, EXPERT_HIDDEN), d ** -0.5),
        "exp_w2": nrm(ks[22], (DEPTH, N_EXPERTS, EXPERT_HIDDEN, d), EXPERT_HIDDEN ** -0.5),
        "shared_w1": nrm(ks[23], (DEPTH, d, SHARED_HIDDEN), d ** -0.5),
        "shared_w3": nrm(ks[24], (DEPTH, d, SHARED_HIDDEN), d ** -0.5),
        "shared_w2": nrm(ks[25], (DEPTH, SHARED_HIDDEN, d), SHARED_HIDDEN ** -0.5),
        "final_g": 1.0 + nrm(ks[26], (d,), 0.02),
    }


def reference(x, c, ctx, c_ctx, mod_w, mod_b, norm1_g, norm2_g, w_in, conv_w, conv_b, conv_ln_g,
              conv_ln_b, w_conv_out, ret_log_gamma_fwd, ret_log_gamma_bwd, w_ret_out, w_merge_out,
              router_w, router_bias, exp_w1, exp_w3, exp_w2, shared_w1, shared_w3, shared_w2, final_g):
    b, s, d = x.shape
    n_ctx = ctx.shape[1]
    cx = ctx
    sc = jax.nn.silu(c)
    scc = jax.nn.silu(c_ctx)
    for i in range(DEPTH):
        last = i == DEPTH - 1
        mod = sc @ mod_w[i] + mod_b[i]
        sh1, sc1, g1, sh2, sc2, g2 = jnp.split(mod[:, None, :], N_MOD, axis=-1)
        n_cm = 2 if last else N_MOD
        mod_c = jnp.split(scc @ mod_w[i][:, :n_cm * d] + mod_b[i][:n_cm * d], n_cm)
        hl = rmsnorm(x, norm1_g[i]) * (1.0 + sc1) + sh1
        hc = rmsnorm(cx, norm1_g[i]) * (1.0 + mod_c[1]) + mod_c[0]
        lat_mix, ctx_mix = mixer(hl, hc, w_in[i], conv_w[i], conv_b[i], conv_ln_g[i], conv_ln_b[i],
                                 w_conv_out[i], ret_log_gamma_fwd[i], ret_log_gamma_bwd[i],
                                 w_ret_out[i], w_merge_out[i], not last)
        x = x + g1 * lat_mix
        hl2 = (rmsnorm(x, norm2_g[i]) * (1.0 + sc2) + sh2).reshape(b * s, d)
        if last:
            tokens = hl2
        else:
            cx = cx + mod_c[2] * ctx_mix
            hc2 = rmsnorm(cx, norm2_g[i]) * (1.0 + mod_c[4]) + mod_c[3]
            tokens = jnp.concatenate([hl2, hc2.reshape(b * n_ctx, d)], axis=0)
        f = moe(tokens, router_w[i], router_bias[i], exp_w1[i], exp_w3[i], exp_w2[i],
                shared_w1[i], shared_w3[i], shared_w2[i])
        x = x + g2 * f[:b * s].reshape(b, s, d)
        if not last:
            cx = cx + mod_c[5] * f[b * s:].reshape(b, n_ctx, d)
    return rmsnorm(x, final_g)
```

```python
import functools

import jax
import jax.numpy as jnp
from jax import lax
from jax.experimental import pallas as pl
from jax.experimental.pallas import tpu as pltpu

GRID_W = 64
N_GROUPS = 8
TOPK_GROUPS = 4
TOP_K = 8
ROUTED_SCALE = 2.5
N_MOD = 6
NORM_EPS = 1e-6
ROPE_BASE = 10000.0

V7X_LANES = 128
V7X_SUBLANES = 8
V7X_VMEM_BYTES = 64 * 1024 * 1024
VMEM_LIMIT = V7X_VMEM_BYTES - 8 * 1024 * 1024

ROW_TILE = 256
HALO = 16
MM_BM = 1024
MM_BN = 512
MOE_TM = 512
HID_CHUNK = 256

BF16 = jnp.bfloat16
F32 = jnp.float32


def _params(*sem):
    return pltpu.CompilerParams(dimension_semantics=sem, vmem_limit_bytes=VMEM_LIMIT)


def _dot(a, b):
    return jnp.dot(a, b, preferred_element_type=F32)


def _dot_nt(a, b):
    return lax.dot_general(a, b, (((1,), (1,)), ((), ())), preferred_element_type=F32)


def _dot_tn(a, b):
    return lax.dot_general(a, b, (((0,), (0,)), ((), ())), preferred_element_type=F32)


def _silu(x):
    return x * jax.nn.sigmoid(x)


def _mod_row(i, n_lat_tiles, tiles_per_seq, n_batch):
    return jnp.where(i < n_lat_tiles, i // tiles_per_seq, n_batch)


def _mod_kernel(c_ref, w_ref, b_ref, o_ref):
    c = c_ref[...]
    sc = _silu(c).astype(BF16)
    o_ref[...] = _dot(sc, w_ref[...].astype(BF16)) + b_ref[...]


def modulation(c8, mod_w, mod_b3, layer):
    d = c8.shape[1]
    n = mod_w.shape[2]
    bn = MM_BN
    return pl.pallas_call(
        _mod_kernel,
        grid=(n // bn,),
        in_specs=[
            pl.BlockSpec((8, d), lambda j: (0, 0)),
            pl.BlockSpec((None, d, bn), lambda j: (layer, 0, j)),
            pl.BlockSpec((None, 1, bn), lambda j: (layer, 0, j)),
        ],
        out_specs=pl.BlockSpec((8, bn), lambda j: (0, j)),
        out_shape=jax.ShapeDtypeStruct((8, n), F32),
        compiler_params=_params("arbitrary"),
        name="modulation",
    )(c8, mod_w, mod_b3)


def _rms(x, g):
    return x * lax.rsqrt(jnp.mean(x * x, axis=-1, keepdims=True) + NORM_EPS) * g


def _norm_mod_kernel(x_ref, g_ref, sh_ref, sc_ref, o_ref):
    h = _rms(x_ref[...], g_ref[...]) * (1.0 + sc_ref[...]) + sh_ref[...]
    o_ref[...] = h.astype(o_ref.dtype)


def norm_mod(x, g3, mod3, layer, sh_idx, sc_idx, rows, n_lat_rows, seq, n_batch):
    d = x.shape[1]
    tr = ROW_TILE
    row = functools.partial(_mod_row, n_lat_tiles=n_lat_rows // tr, tiles_per_seq=seq // tr, n_batch=n_batch)
    return pl.pallas_call(
        _norm_mod_kernel,
        grid=(rows // tr,),
        in_specs=[
            pl.BlockSpec((tr, d), lambda i: (i, 0)),
            pl.BlockSpec((None, 1, d), lambda i: (layer, 0, 0)),
            pl.BlockSpec((None, 1, d), lambda i: (row(i) * N_MOD + sh_idx, 0, 0)),
            pl.BlockSpec((None, 1, d), lambda i: (row(i) * N_MOD + sc_idx, 0, 0)),
        ],
        out_specs=pl.BlockSpec((tr, d), lambda i: (i, 0)),
        out_shape=jax.ShapeDtypeStruct((rows, d), BF16),
        compiler_params=_params("parallel"),
        name="norm_mod",
    )(x, g3, mod3, mod3)


def _final_norm_kernel(x_ref, g_ref, o_ref):
    o_ref[...] = _rms(x_ref[...], g_ref[...])


def final_norm(x, g2, rows):
    d = x.shape[1]
    tr = ROW_TILE
    return pl.pallas_call(
        _final_norm_kernel,
        grid=(rows // tr,),
        in_specs=[pl.BlockSpec((tr, d), lambda i: (i, 0)), pl.BlockSpec((1, d), lambda i: (0, 0))],
        out_specs=pl.BlockSpec((tr, d), lambda i: (i, 0)),
        out_shape=jax.ShapeDtypeStruct((rows, d), F32),
        compiler_params=_params("parallel"),
        name="final_norm",
    )(x, g2)


def _route(logits_t, bias):
    n_exp, tt = logits_t.shape
    per = n_exp // N_GROUPS
    s = jax.nn.sigmoid(logits_t)
    biased = s + bias
    g3 = biased.reshape(N_GROUPS, per, tt)
    mem = lax.broadcasted_iota(jnp.int32, g3.shape, 1)
    m1 = jnp.max(g3, axis=1, keepdims=True)
    first = jnp.min(jnp.where(g3 == m1, mem, per), axis=1, keepdims=True)
    m2 = jnp.max(jnp.where(mem == first, -jnp.inf, g3), axis=1, keepdims=True)
    gs = m1 + m2
    gid = lax.broadcasted_iota(jnp.int32, gs.shape, 0)
    grank = jnp.zeros(gs.shape, jnp.int32)
    for j in range(N_GROUPS):
        other = gs[j:j + 1]
        ahead = (other > gs) | ((other == gs) & (gid > j))
        grank = grank + ahead.astype(jnp.int32)
    gmask = grank < TOPK_GROUPS
    masked = jnp.where(gmask, g3, -jnp.inf).reshape(n_exp, tt)
    eid = lax.broadcasted_iota(jnp.int32, masked.shape, 0)
    rank = jnp.zeros(masked.shape, jnp.int32)
    for j in range(n_exp):
        other = masked[j:j + 1, :]
        ahead = (other > masked) | ((other == masked) & (eid > j))
        rank = rank + ahead.astype(jnp.int32)
    sel = jnp.where(rank < TOP_K, s, 0.0)
    return sel / jnp.sum(sel, axis=0, keepdims=True) * ROUTED_SCALE


def _norm_router_kernel(x_ref, g_ref, sh_ref, sc_ref, wcat_ref, whi_ref, bias_ref, h_ref, comb_ref):
    h = _rms(x_ref[...], g_ref[...]) * (1.0 + sc_ref[...]) + sh_ref[...]
    h_hi = h.astype(BF16)
    h_lo = (h - h_hi.astype(F32)).astype(BF16)
    h_ref[...] = h_hi
    n_exp = whi_ref.shape[0]
    lt = _dot_nt(wcat_ref[...], h_hi)
    logits_t = lt[:n_exp] + lt[n_exp:] + _dot_nt(whi_ref[...], h_lo)
    comb_ref[...] = _route(logits_t, bias_ref[...]).T


def norm_router(x, g3, mod3, wcat_t, whi_t, bias3, layer, sh_idx, sc_idx, rows, n_lat_rows, seq, n_batch):
    d = x.shape[1]
    n_exp = whi_t.shape[1]
    tr = ROW_TILE
    row = functools.partial(_mod_row, n_lat_tiles=n_lat_rows // tr, tiles_per_seq=seq // tr, n_batch=n_batch)
    return pl.pallas_call(
        _norm_router_kernel,
        grid=(rows // tr,),
        in_specs=[
            pl.BlockSpec((tr, d), lambda i: (i, 0)),
            pl.BlockSpec((None, 1, d), lambda i: (layer, 0, 0)),
            pl.BlockSpec((None, 1, d), lambda i: (row(i) * N_MOD + sh_idx, 0, 0)),
            pl.BlockSpec((None, 1, d), lambda i: (row(i) * N_MOD + sc_idx, 0, 0)),
            pl.BlockSpec((None, 2 * n_exp, d), lambda i: (layer, 0, 0)),
            pl.BlockSpec((None, n_exp, d), lambda i: (layer, 0, 0)),
            pl.BlockSpec((None, n_exp, 1), lambda i: (layer, 0, 0)),
        ],
        out_specs=[
            pl.BlockSpec((tr, d), lambda i: (i, 0)),
            pl.BlockSpec((tr, n_exp), lambda i: (i, 0)),
        ],
        out_shape=[
            jax.ShapeDtypeStruct((rows, d), BF16),
            jax.ShapeDtypeStruct((rows, n_exp), F32),
        ],
        compiler_params=_params("parallel"),
        name="norm_router",
    )(x, g3, mod3, mod3, wcat_t, whi_t, bias3)


def _mm_kernel(a_ref, w_ref, o_ref, wb_ref):
    @pl.when(pl.program_id(1) == 0)
    def _():
        wb_ref[...] = w_ref[...].astype(BF16)

    o_ref[...] = _dot(a_ref[...], wb_ref[...]).astype(o_ref.dtype)


def matmul(a, w, layer, rows, row_off, col_off, ncols, bm, out_dtype=F32):
    k = a.shape[1]
    bn = MM_BN
    ro, co = row_off // bm, col_off // bn
    return pl.pallas_call(
        _mm_kernel,
        grid=(ncols // bn, rows // bm),
        in_specs=[
            pl.BlockSpec((bm, k), lambda j, i: (i + ro, 0)),
            pl.BlockSpec((None, k, bn), lambda j, i: (layer, 0, j + co)),
        ],
        out_specs=pl.BlockSpec((bm, bn), lambda j, i: (i, j)),
        out_shape=jax.ShapeDtypeStruct((rows, ncols), out_dtype),
        scratch_shapes=[pltpu.VMEM((k, bn), BF16)],
        compiler_params=_params("arbitrary", "arbitrary"),
        name="matmul",
    )(a, w)


def _merge_kernel(yc_ref, yr_ref, gc_ref, gr_ref, wc_ref, wr_ref, o_ref, wcb_ref, wrb_ref):
    @pl.when(pl.program_id(1) == 0)
    def _():
        wcb_ref[...] = wc_ref[...].astype(BF16)
        wrb_ref[...] = wr_ref[...].astype(BF16)

    y = jax.nn.sigmoid(gc_ref[...]) * _dot(yc_ref[...], wcb_ref[...])
    y = y + jax.nn.sigmoid(gr_ref[...]) * _dot(yr_ref[...], wrb_ref[...])
    o_ref[...] = y.astype(o_ref.dtype)


def merge_branches(yc, yr, u, w_conv_out, w_ret_out, layer, rows, gate_off, bm):
    kc, kr = yc.shape[1], yr.shape[1]
    d = w_conv_out.shape[2]
    bn = MM_BN
    gco, gro = gate_off // bn, (gate_off + d) // bn
    return pl.pallas_call(
        _merge_kernel,
        grid=(d // bn, rows // bm),
        in_specs=[
            pl.BlockSpec((bm, kc), lambda j, i: (i, 0)),
            pl.BlockSpec((bm, kr), lambda j, i: (i, 0)),
            pl.BlockSpec((bm, bn), lambda j, i: (i, j + gco)),
            pl.BlockSpec((bm, bn), lambda j, i: (i, j + gro)),
            pl.BlockSpec((None, kc, bn), lambda j, i: (layer, 0, j)),
            pl.BlockSpec((None, kr, bn), lambda j, i: (layer, 0, j)),
        ],
        out_specs=pl.BlockSpec((bm, bn), lambda j, i: (i, j)),
        out_shape=jax.ShapeDtypeStruct((rows, d), BF16),
        scratch_shapes=[pltpu.VMEM((kc, bn), BF16), pltpu.VMEM((kr, bn), BF16)],
        compiler_params=_params("arbitrary", "arbitrary"),
        name="merge_branches",
    )(yc, yr, u, u, w_conv_out, w_ret_out)


def _resid_mm_kernel(a_ref, w_ref, x_ref, g_ref, o_ref, wb_ref):
    @pl.when(pl.program_id(1) == 0)
    def _():
        wb_ref[...] = w_ref[...].astype(BF16)

    o_ref[...] = x_ref[...] + g_ref[...] * _dot(a_ref[...], wb_ref[...])


def _resid_mm_add_kernel(a_ref, w_ref, x_ref, g_ref, e_ref, o_ref, wb_ref):
    @pl.when(pl.program_id(1) == 0)
    def _():
        wb_ref[...] = w_ref[...].astype(BF16)

    o_ref[...] = x_ref[...] + g_ref[...] * (e_ref[...] + _dot(a_ref[...], wb_ref[...]))


def resid_matmul(a, w, x, mod3, layer, g_idx, rows, n_lat_rows, seq, n_batch, bm, extra=None):
    k = a.shape[1]
    d = w.shape[2]
    bn = MM_BN
    row = functools.partial(_mod_row, n_lat_tiles=n_lat_rows // bm, tiles_per_seq=seq // bm, n_batch=n_batch)
    tile = pl.BlockSpec((bm, bn), lambda j, i: (i, j))
    in_specs = [
        pl.BlockSpec((bm, k), lambda j, i: (i, 0)),
        pl.BlockSpec((None, k, bn), lambda j, i: (layer, 0, j)),
        tile,
        pl.BlockSpec((None, 1, bn), lambda j, i: (row(i) * N_MOD + g_idx, 0, j)),
    ]
    operands = [a, w, x, mod3]
    if extra is not None:
        in_specs.append(tile)
        operands.append(extra)
    return pl.pallas_call(
        _resid_mm_kernel if extra is None else _resid_mm_add_kernel,
        grid=(d // bn, rows // bm),
        in_specs=in_specs,
        out_specs=tile,
        out_shape=jax.ShapeDtypeStruct((rows, d), F32),
        scratch_shapes=[pltpu.VMEM((k, bn), BF16)],
        compiler_params=_params("arbitrary", "arbitrary"),
        name="resid_matmul",
    )(*operands)


def _conv_kernel(a_ref, g_ref, ap_ref, gp_ref, an_ref, gn_ref, w_ref, b_ref, lng_ref, lnb_ref, o_ref,
                 buf_ref, acc_ref, *, n_lat_tiles, lat_tps, ctx_tps, conv_k):
    i = pl.program_id(0)
    tr, cc = a_ref.shape
    is_lat = i < n_lat_tiles
    pos = jnp.where(is_lat, i % lat_tps, (i - n_lat_tiles) % ctx_tps)
    tps = jnp.where(is_lat, lat_tps, ctx_tps)
    keep_prev = (pos != 0).astype(F32)
    keep_next = (pos != tps - 1).astype(F32)
    buf_ref[HALO:HALO + tr, :] = a_ref[...] * jax.nn.sigmoid(g_ref[...])
    buf_ref[0:HALO, :] = ap_ref[...] * jax.nn.sigmoid(gp_ref[...]) * keep_prev
    buf_ref[HALO + tr:, :] = an_ref[...] * jax.nn.sigmoid(gn_ref[...]) * keep_next
    pad = (conv_k - 1) // 2
    rc = 128
    n_rc = tr // rc

    def chunk(t, carry):
        lanes = pl.ds(pl.multiple_of(t * V7X_LANES, V7X_LANES), V7X_LANES)
        for r in range(n_rc):
            acc = jnp.zeros((rc, V7X_LANES), F32)
            for kk in range(conv_k):
                start = r * rc + HALO - pad + kk
                acc = acc + buf_ref[start:start + rc, lanes] * w_ref[kk:kk + 1, lanes]
            acc_ref[r * rc:(r + 1) * rc, lanes] = acc
        return carry

    lax.fori_loop(0, cc // V7X_LANES, chunk, 0)
    y = acc_ref[...] + b_ref[...]
    mu = jnp.mean(y, axis=-1, keepdims=True)
    yc = y - mu
    var = jnp.mean(yc * yc, axis=-1, keepdims=True)
    z = yc * lax.rsqrt(var + NORM_EPS) * lng_ref[...] + lnb_ref[...]
    o_ref[...] = _silu(z).astype(o_ref.dtype)


def conv_branch(u, conv_w, conv_b3, ln_g3, ln_b3, layer, rows, n_lat_rows, seq, ctx_len):
    conv_k, cc = conv_w.shape[1], conv_w.shape[2]
    tr = ROW_TILE
    hb = tr // HALO
    n_halo_blocks = u.shape[0] // HALO
    kern = functools.partial(_conv_kernel, n_lat_tiles=n_lat_rows // tr, lat_tps=seq // tr,
                             ctx_tps=max(ctx_len // tr, 1), conv_k=conv_k)
    prev = lambda i: jnp.maximum(i * hb - 1, 0)
    nxt = lambda i: jnp.minimum((i + 1) * hb, n_halo_blocks - 1)
    return pl.pallas_call(
        kern,
        grid=(rows // tr,),
        in_specs=[
            pl.BlockSpec((tr, cc), lambda i: (i, 0)),
            pl.BlockSpec((tr, cc), lambda i: (i, 1)),
            pl.BlockSpec((HALO, cc), lambda i: (prev(i), 0)),
            pl.BlockSpec((HALO, cc), lambda i: (prev(i), 1)),
            pl.BlockSpec((HALO, cc), lambda i: (nxt(i), 0)),
            pl.BlockSpec((HALO, cc), lambda i: (nxt(i), 1)),
            pl.BlockSpec((None, conv_k, cc), lambda i: (layer, 0, 0)),
            pl.BlockSpec((None, 1, cc), lambda i: (layer, 0, 0)),
            pl.BlockSpec((None, 1, cc), lambda i: (layer, 0, 0)),
            pl.BlockSpec((None, 1, cc), lambda i: (layer, 0, 0)),
        ],
        out_specs=pl.BlockSpec((tr, cc), lambda i: (i, 0)),
        out_shape=jax.ShapeDtypeStruct((rows, cc), BF16),
        scratch_shapes=[pltpu.VMEM((tr + 2 * HALO, cc), F32), pltpu.VMEM((tr, cc), F32)],
        compiler_params=_params("parallel"),
        name="conv_branch",
    )(u, u, u, u, u, u, conv_w, conv_b3, ln_g3, ln_b3)


def _rope(x, cos, sin):
    half = V7X_LANES
    parts = []
    for p in range(x.shape[1] // half):
        xp = x[:, p * half:(p + 1) * half]
        parts.append(pltpu.roll(xp, half // 2, 1))
    return x * cos + jnp.concatenate(parts, axis=1) * sin


def _decay_terms(lgf, lgb, c):
    ia = lax.broadcasted_iota(jnp.int32, (c, c), 0)
    ib = lax.broadcasted_iota(jnp.int32, (c, c), 1)
    rel = (ia - ib).astype(F32)
    dm = jnp.where(rel >= 0, jnp.exp(lgf * jnp.maximum(rel, 0.0)), 0.0)
    dm = dm + jnp.where(rel <= 0, jnp.exp(lgb * jnp.maximum(-rel, 0.0)), 0.0)
    pos = lax.broadcasted_iota(jnp.int32, (c, 1), 0).astype(F32)
    return dm, pos


def _ret_kernel(lgf_ref, lgb_ref, q_ref, k_ref, v_ref, g_ref, cos_ref, sin_ref, s0f_ref, s0b_ref,
                y_ref, stf_ref, stb_ref, o_scr, q_scr, k_scr, sf_ref, sb_ref, *, layer, n_heads, rope, zero_init):
    h = pl.program_id(1)
    t, dh = q_ref.shape
    c = ROW_TILE
    n = t // c
    lgf = lgf_ref[layer * n_heads + h]
    lgb = lgb_ref[layer * n_heads + h]
    dm, pos = _decay_terms(lgf, lgb, c)
    xi_f = jnp.exp(lgf * (pos + 1.0))
    zeta_f = jnp.exp(lgf * (c - 1.0 - pos))
    xi_b = jnp.exp(lgb * (c - pos))
    zeta_b = jnp.exp(lgb * pos)
    full = jnp.full((1, 1), float(c), F32)
    cd_f = jnp.exp(lgf * full)
    cd_b = jnp.exp(lgb * full)
    scale = dh ** -0.5
    if zero_init:
        sf_ref[...] = jnp.zeros_like(sf_ref)
        sb_ref[...] = jnp.zeros_like(sb_ref)
    else:
        sf_ref[...] = s0f_ref[...]
        sb_ref[...] = s0b_ref[...]

    def fwd(i, carry):
        r = pl.ds(pl.multiple_of(i * c, c), c)
        q = q_ref[r, :]
        k = k_ref[r, :]
        if rope:
            q = _rope(q, cos_ref[r, :], sin_ref[r, :])
            k = _rope(k, cos_ref[r, :], sin_ref[r, :])
        qb = (q * scale).astype(BF16)
        vb = v_ref[r, :].astype(BF16)
        q_scr[r, :] = qb
        k_scr[r, :] = k
        s = _dot_nt(qb, k.astype(BF16)) * dm
        o = _dot(s.astype(BF16), vb) + xi_f * _dot(qb, sf_ref[...].astype(BF16))
        o_scr[r, :] = o
        sf_ref[...] = sf_ref[...] * cd_f + _dot_tn((k * zeta_f).astype(BF16), vb)
        return carry

    lax.fori_loop(0, n, fwd, 0)

    def bwd(j, carry):
        i = n - 1 - j
        r = pl.ds(pl.multiple_of(i * c, c), c)
        qb = q_scr[r, :]
        vb = v_ref[r, :].astype(BF16)
        o = o_scr[r, :] + xi_b * _dot(qb, sb_ref[...].astype(BF16))
        sb_ref[...] = sb_ref[...] * cd_b + _dot_tn((k_scr[r, :] * zeta_b).astype(BF16), vb)
        o = o * lax.rsqrt(jnp.mean(o * o, axis=-1, keepdims=True) + NORM_EPS)
        y_ref[r, :] = (o * _silu(g_ref[r, :])).astype(y_ref.dtype)
        return carry

    lax.fori_loop(0, n, bwd, 0)
    stf_ref[...] = sf_ref[...]
    stb_ref[...] = sb_ref[...]


def retention(lgf, lgb, u, cos, sin, s0f, s0b, layer, n_batch, t, row_off, q_off, n_heads, dh, rope, zero_init):
    rw = n_heads * dh
    ro = row_off // t
    qo, ko, vo, go = ((q_off + m * rw) // dh for m in range(4))
    kern = functools.partial(_ret_kernel, layer=layer, n_heads=n_heads, rope=rope, zero_init=zero_init)
    col = lambda off: pl.BlockSpec((t, dh), lambda b, h, *_: (b + ro, off + h))
    tab = pl.BlockSpec((t, dh), lambda b, h, *_: (0, 0))
    st = pl.BlockSpec((None, None, dh, dh), lambda b, h, *_: (b, h, 0, 0))
    grid_spec = pltpu.PrefetchScalarGridSpec(
        num_scalar_prefetch=2,
        grid=(n_batch, n_heads),
        in_specs=[col(qo), col(ko), col(vo), col(go), tab, tab, st, st],
        out_specs=[pl.BlockSpec((t, dh), lambda b, h, *_: (b, h)), st, st],
        scratch_shapes=[
            pltpu.VMEM((t, dh), F32), pltpu.VMEM((t, dh), BF16), pltpu.VMEM((t, dh), F32),
            pltpu.VMEM((dh, dh), F32), pltpu.VMEM((dh, dh), F32),
        ],
    )
    st_shape = jax.ShapeDtypeStruct((n_batch, n_heads, dh, dh), F32)
    return pl.pallas_call(
        kern,
        grid_spec=grid_spec,
        out_shape=[jax.ShapeDtypeStruct((n_batch * t, rw), BF16), st_shape, st_shape],
        compiler_params=_params("parallel", "parallel"),
        name="retention",
    )(lgf, lgb, u, u, u, u, cos, sin, s0f, s0b)


def _ctx_state_kernel(lgf_ref, lgb_ref, k_ref, v_ref, stf_ref, stb_ref, *, layer, n_heads):
    h = pl.program_id(1)
    t = k_ref.shape[0]
    lgf = lgf_ref[layer * n_heads + h]
    lgb = lgb_ref[layer * n_heads + h]
    pos = lax.broadcasted_iota(jnp.int32, (t, 1), 0).astype(F32)
    k = k_ref[...]
    vb = v_ref[...].astype(BF16)
    stf_ref[...] = _dot_tn((k * jnp.exp(lgf * (t - 1.0 - pos))).astype(BF16), vb)
    stb_ref[...] = _dot_tn((k * jnp.exp(lgb * pos)).astype(BF16), vb)


def ctx_states(lgf, lgb, kv, layer, n_batch, t, n_heads, dh):
    kern = functools.partial(_ctx_state_kernel, layer=layer, n_heads=n_heads)
    st = pl.BlockSpec((None, None, dh, dh), lambda b, h, *_: (b, h, 0, 0))
    grid_spec = pltpu.PrefetchScalarGridSpec(
        num_scalar_prefetch=2,
        grid=(n_batch, n_heads),
        in_specs=[
            pl.BlockSpec((t, dh), lambda b, h, *_: (b, h)),
            pl.BlockSpec((t, dh), lambda b, h, *_: (b, n_heads + h)),
        ],
        out_specs=[st, st],
    )
    st_shape = jax.ShapeDtypeStruct((n_batch, n_heads, dh, dh), F32)
    return pl.pallas_call(
        kern,
        grid_spec=grid_spec,
        out_shape=[st_shape, st_shape],
        compiler_params=_params("parallel", "parallel"),
        name="ctx_states",
    )(lgf, lgb, kv, kv)


def _moe_dense_kernel(h_ref, comb_ref, w1_ref, w3_ref, w2_ref, o_ref):
    e = pl.program_id(1)
    h = h_ref[...]
    a1 = _dot(h, w1_ref[...].astype(BF16))
    a3 = _dot(h, w3_ref[...].astype(BF16))
    comb = comb_ref[...]
    lane = lax.broadcasted_iota(jnp.int32, comb.shape, 1)
    ce = jnp.sum(jnp.where(lane == e, comb, 0.0), axis=1, keepdims=True)
    a = _silu(a1) * a3 * ce
    y = _dot(a.astype(BF16), w2_ref[...].astype(BF16))

    @pl.when(e == 0)
    def _():
        o_ref[...] = y

    @pl.when(e > 0)
    def _():
        o_ref[...] += y


def moe_routed_dense(h, comb, w1, w3, w2, layer, rows):
    d = h.shape[1]
    n_exp, hid = w1.shape[1], w1.shape[3]
    tm = MOE_TM
    return pl.pallas_call(
        _moe_dense_kernel,
        grid=(rows // tm, n_exp),
        in_specs=[
            pl.BlockSpec((tm, d), lambda i, e: (i, 0)),
            pl.BlockSpec((tm, n_exp), lambda i, e: (i, 0)),
            pl.BlockSpec((None, None, d, hid), lambda i, e: (layer, e, 0, 0)),
            pl.BlockSpec((None, None, d, hid), lambda i, e: (layer, e, 0, 0)),
            pl.BlockSpec((None, None, hid, d), lambda i, e: (layer, e, 0, 0)),
        ],
        out_specs=pl.BlockSpec((tm, d), lambda i, e: (i, 0)),
        out_shape=jax.ShapeDtypeStruct((rows, d), F32),
        compiler_params=_params("parallel", "arbitrary"),
        name="moe_routed_dense",
    )(h, comb, w1, w3, w2)


def _glu_mm_kernel(h_ref, w1_ref, w3_ref, o_ref, w1b_ref, w3b_ref):
    @pl.when(pl.program_id(1) == 0)
    def _():
        w1b_ref[...] = w1_ref[...].astype(BF16)
        w3b_ref[...] = w3_ref[...].astype(BF16)

    h = h_ref[...]
    o_ref[...] = (_silu(_dot(h, w1b_ref[...])) * _dot(h, w3b_ref[...])).astype(o_ref.dtype)


def glu_matmul(h, w1, w3, layer, rows, bm):
    k = h.shape[1]
    n = w1.shape[2]
    bn = HID_CHUNK
    return pl.pallas_call(
        _glu_mm_kernel,
        grid=(n // bn, rows // bm),
        in_specs=[
            pl.BlockSpec((bm, k), lambda j, i: (i, 0)),
            pl.BlockSpec((None, k, bn), lambda j, i: (layer, 0, j)),
            pl.BlockSpec((None, k, bn), lambda j, i: (layer, 0, j)),
        ],
        out_specs=pl.BlockSpec((bm, bn), lambda j, i: (i, j)),
        out_shape=jax.ShapeDtypeStruct((rows, n), BF16),
        scratch_shapes=[pltpu.VMEM((k, bn), BF16), pltpu.VMEM((k, bn), BF16)],
        compiler_params=_params("arbitrary", "arbitrary"),
        name="glu_matmul",
    )(h, w1, w3)


def _rope_tables(seq, dh):
    rows = seq // GRID_W
    row = jnp.repeat(jnp.arange(rows, dtype=F32), GRID_W)
    col = jnp.tile(jnp.arange(GRID_W, dtype=F32), rows)
    quarter = dh // 4
    inv_freq = ROPE_BASE ** (-jnp.arange(quarter, dtype=F32) / quarter)
    ang_r = row[:, None] * inv_freq[None, :]
    ang_c = col[:, None] * inv_freq[None, :]
    cos = jnp.concatenate([jnp.cos(ang_r), jnp.cos(ang_r), jnp.cos(ang_c), jnp.cos(ang_c)], axis=-1)
    sin = jnp.concatenate([-jnp.sin(ang_r), jnp.sin(ang_r), -jnp.sin(ang_c), jnp.sin(ang_c)], axis=-1)
    return cos, sin


def kernel(x, c, ctx, c_ctx, mod_w, mod_b, norm1_g, norm2_g, w_in, conv_w, conv_b, conv_ln_g, conv_ln_b,
           w_conv_out, ret_log_gamma_fwd, ret_log_gamma_bwd, w_ret_out, w_merge_out, router_w, router_bias,
           exp_w1, exp_w3, exp_w2, shared_w1, shared_w3, shared_w2, final_g):
    b, s, d = x.shape
    n_ctx = ctx.shape[1]
    depth = mod_w.shape[0]
    cc = conv_w.shape[2]
    n_heads = ret_log_gamma_fwd.shape[1]
    rw = w_ret_out.shape[1]
    dh = rw // n_heads
    n_exp = router_w.shape[2]
    n_lat, n_cx = b * s, b * n_ctx
    n_all = n_lat + n_cx
    bm = min(MM_BM, s, n_cx)
    assert b + 1 <= 8 and s % ROW_TILE == 0 and n_ctx % ROW_TILE == 0 and dh == 2 * V7X_LANES
    assert s % bm == 0 and n_cx % bm == 0 and s % MOE_TM == 0 and n_cx % MOE_TM == 0
    q_off, gate_off = 2 * cc, 2 * cc + 4 * rw

    xa = jnp.concatenate([x.reshape(n_lat, d), ctx.reshape(n_cx, d)], axis=0)
    c8 = jnp.zeros((8, d), F32).at[:b].set(c).at[b].set(c_ctx)
    mod_b3 = mod_b.reshape(depth, 1, -1)
    n1g, n2g = norm1_g.reshape(depth, 1, d), norm2_g.reshape(depth, 1, d)
    conv_b3, ln_g3, ln_b3 = (a.reshape(depth, 1, cc) for a in (conv_b, conv_ln_g, conv_ln_b))
    lgf, lgb = ret_log_gamma_fwd.reshape(-1), ret_log_gamma_bwd.reshape(-1)
    rw_t = jnp.swapaxes(router_w, 1, 2)
    rw_hi = rw_t.astype(BF16)
    rw_lo = (rw_t - rw_hi.astype(F32)).astype(BF16)
    rw_cat = jnp.concatenate([rw_hi, rw_lo], axis=1)
    bias3 = router_bias.reshape(depth, n_exp, 1)
    cos, sin = _rope_tables(s, dh)
    zero_st = jnp.zeros((b, n_heads, dh, dh), F32)
    ew1, ew3, ew2 = exp_w1.astype(BF16), exp_w3.astype(BF16), exp_w2.astype(BF16)

    for layer in range(depth):
        last = layer == depth - 1
        rows = n_lat if last else n_all
        seq_args = dict(n_lat_rows=n_lat, seq=s, n_batch=b)
        mod3 = modulation(c8, mod_w, mod_b3, layer).reshape(8 * N_MOD, 1, d)
        h1 = norm_mod(xa, n1g, mod3, layer, 0, 1, n_all, **seq_args)
        if last:
            u = matmul(h1, w_in, layer, n_lat, 0, 0, w_in.shape[2], bm)
            kv_c = matmul(h1, w_in, layer, n_cx, n_lat, q_off + rw, 2 * rw, bm)
            st_f, st_b = ctx_states(lgf, lgb, kv_c, layer, b, n_ctx, n_heads, dh)
        else:
            u = matmul(h1, w_in, layer, n_all, 0, 0, w_in.shape[2], bm)
            yr_c, st_f, st_b = retention(lgf, lgb, u, cos, sin, zero_st, zero_st, layer, b, n_ctx, n_lat, q_off,
                                         n_heads, dh, rope=False, zero_init=True)
        yc = conv_branch(u, conv_w, conv_b3, ln_g3, ln_b3, layer, rows, n_lat, s, n_ctx)
        yr, _, _ = retention(lgf, lgb, u, cos, sin, st_f, st_b, layer, b, s, 0, q_off, n_heads, dh,
                             rope=True, zero_init=False)
        if not last:
            yr = jnp.concatenate([yr, yr_c], axis=0)
        mixed = merge_branches(yc, yr, u, w_conv_out, w_ret_out, layer, rows, gate_off, bm)
        xa = resid_matmul(mixed, w_merge_out, xa, mod3, layer, 2, rows, bm=bm, **seq_args)
        h2, comb = norm_router(xa, n2g, mod3, rw_cat, rw_hi, bias3, layer, 3, 4, rows, **seq_args)
        routed = moe_routed_dense(h2, comb, ew1, ew3, ew2, layer, rows)
        act = glu_matmul(h2, shared_w1, shared_w3, layer, rows, MOE_TM)
        xa = resid_matmul(act, shared_w2, xa, mod3, layer, 5, rows, bm=bm, extra=routed, **seq_args)
    out = final_norm(xa, final_g.reshape(1, d), n_lat)
    return out.reshape(b, s, d)
```

```python
import functools

import jax
import jax.numpy as jnp
from jax import lax
from jax.experimental import pallas as pl
from jax.experimental.pallas import tpu as pltpu

GRID_W = 64
N_GROUPS = 8
TOPK_GROUPS = 4
TOP_K = 8
ROUTED_SCALE = 2.5
N_MOD = 6
NORM_EPS = 1e-6
ROPE_BASE = 10000.0

V7X_LANES = 128
V7X_SUBLANES = 8
V7X_VMEM_BYTES = 64 * 1024 * 1024
VMEM_LIMIT = V7X_VMEM_BYTES - 8 * 1024 * 1024

ROW_TILE = 256
HALO = 16
MM_BM = 1024
MM_BN = 512
MOE_TM = 512
MOE_BMX = 256
COMBINE_TT = 32
HID_CHUNK = 256

BF16 = jnp.bfloat16
F32 = jnp.float32


def _params(*sem):
    return pltpu.CompilerParams(dimension_semantics=sem, vmem_limit_bytes=VMEM_LIMIT)


def _dot(a, b):
    return jnp.dot(a, b, preferred_element_type=F32)


def _dot_nt(a, b):
    return lax.dot_general(a, b, (((1,), (1,)), ((), ())), preferred_element_type=F32)


def _dot_tn(a, b):
    return lax.dot_general(a, b, (((0,), (0,)), ((), ())), preferred_element_type=F32)


def _silu(x):
    return x * jax.nn.sigmoid(x)


def _mod_row(i, n_lat_tiles, tiles_per_seq, n_batch):
    return jnp.where(i < n_lat_tiles, i // tiles_per_seq, n_batch)


def _mod_kernel(c_ref, w_ref, b_ref, o_ref):
    c = c_ref[...]
    sc = _silu(c).astype(BF16)
    o_ref[...] = _dot(sc, w_ref[...].astype(BF16)) + b_ref[...]


def modulation(c8, mod_w, mod_b3, layer):
    d = c8.shape[1]
    n = mod_w.shape[2]
    bn = MM_BN
    return pl.pallas_call(
        _mod_kernel,
        grid=(n // bn,),
        in_specs=[
            pl.BlockSpec((8, d), lambda j: (0, 0)),
            pl.BlockSpec((None, d, bn), lambda j: (layer, 0, j)),
            pl.BlockSpec((None, 1, bn), lambda j: (layer, 0, j)),
        ],
        out_specs=pl.BlockSpec((8, bn), lambda j: (0, j)),
        out_shape=jax.ShapeDtypeStruct((8, n), F32),
        compiler_params=_params("arbitrary"),
        name="modulation",
    )(c8, mod_w, mod_b3)


def _rms(x, g):
    return x * lax.rsqrt(jnp.mean(x * x, axis=-1, keepdims=True) + NORM_EPS) * g


def _norm_mod_kernel(x_ref, g_ref, sh_ref, sc_ref, o_ref):
    h = _rms(x_ref[...], g_ref[...]) * (1.0 + sc_ref[...]) + sh_ref[...]
    o_ref[...] = h.astype(o_ref.dtype)


def norm_mod(x, g3, mod3, layer, sh_idx, sc_idx, rows, n_lat_rows, seq, n_batch):
    d = x.shape[1]
    tr = ROW_TILE
    row = functools.partial(_mod_row, n_lat_tiles=n_lat_rows // tr, tiles_per_seq=seq // tr, n_batch=n_batch)
    return pl.pallas_call(
        _norm_mod_kernel,
        grid=(rows // tr,),
        in_specs=[
            pl.BlockSpec((tr, d), lambda i: (i, 0)),
            pl.BlockSpec((None, 1, d), lambda i: (layer, 0, 0)),
            pl.BlockSpec((None, 1, d), lambda i: (row(i) * N_MOD + sh_idx, 0, 0)),
            pl.BlockSpec((None, 1, d), lambda i: (row(i) * N_MOD + sc_idx, 0, 0)),
        ],
        out_specs=pl.BlockSpec((tr, d), lambda i: (i, 0)),
        out_shape=jax.ShapeDtypeStruct((rows, d), BF16),
        compiler_params=_params("parallel"),
        name="norm_mod",
    )(x, g3, mod3, mod3)


def _final_norm_kernel(x_ref, g_ref, o_ref):
    o_ref[...] = _rms(x_ref[...], g_ref[...])


def final_norm(x, g2, rows):
    d = x.shape[1]
    tr = ROW_TILE
    return pl.pallas_call(
        _final_norm_kernel,
        grid=(rows // tr,),
        in_specs=[pl.BlockSpec((tr, d), lambda i: (i, 0)), pl.BlockSpec((1, d), lambda i: (0, 0))],
        out_specs=pl.BlockSpec((tr, d), lambda i: (i, 0)),
        out_shape=jax.ShapeDtypeStruct((rows, d), F32),
        compiler_params=_params("parallel"),
        name="final_norm",
    )(x, g2)


def _route(logits_t, bias):
    n_exp, tt = logits_t.shape
    per = n_exp // N_GROUPS
    s = jax.nn.sigmoid(logits_t)
    biased = s + bias
    g3 = biased.reshape(N_GROUPS, per, tt)
    mem = lax.broadcasted_iota(jnp.int32, g3.shape, 1)
    m1 = jnp.max(g3, axis=1, keepdims=True)
    first = jnp.min(jnp.where(g3 == m1, mem, per), axis=1, keepdims=True)
    m2 = jnp.max(jnp.where(mem == first, -jnp.inf, g3), axis=1, keepdims=True)
    gs = m1 + m2
    gid = lax.broadcasted_iota(jnp.int32, gs.shape, 0)
    grank = jnp.zeros(gs.shape, jnp.int32)
    for j in range(N_GROUPS):
        other = gs[j:j + 1]
        ahead = (other > gs) | ((other == gs) & (gid > j))
        grank = grank + ahead.astype(jnp.int32)
    gmask = grank < TOPK_GROUPS
    masked = jnp.where(gmask, g3, -jnp.inf).reshape(n_exp, tt)
    eid = lax.broadcasted_iota(jnp.int32, masked.shape, 0)
    rank = jnp.zeros(masked.shape, jnp.int32)
    for j in range(n_exp):
        other = masked[j:j + 1, :]
        ahead = (other > masked) | ((other == masked) & (eid > j))
        rank = rank + ahead.astype(jnp.int32)
    sel = jnp.where(rank < TOP_K, s, 0.0)
    return sel / jnp.sum(sel, axis=0, keepdims=True) * ROUTED_SCALE, rank


def _norm_router_kernel(x_ref, g_ref, sh_ref, sc_ref, wcat_ref, whi_ref, bias_ref,
                        h_ref, eidx_ref, pos_ref, wts_ref, cnt_ref):
    i = pl.program_id(0)
    h = _rms(x_ref[...], g_ref[...]) * (1.0 + sc_ref[...]) + sh_ref[...]
    h_hi = h.astype(BF16)
    h_lo = (h - h_hi.astype(F32)).astype(BF16)
    h_ref[...] = h_hi.astype(F32)
    n_exp = whi_ref.shape[0]
    lt = _dot_nt(wcat_ref[...], h_hi)
    logits_t = lt[:n_exp] + lt[n_exp:] + _dot_nt(whi_ref[...], h_lo)
    comb, rank = _route(logits_t, bias_ref[...])
    tt = comb.shape[1]

    @pl.when(i == 0)
    def _():
        cnt_ref[...] = jnp.zeros_like(cnt_ref)

    chosen = (rank < TOP_K).astype(F32)
    ta = lax.broadcasted_iota(jnp.int32, (tt, tt), 0)
    tb = lax.broadcasted_iota(jnp.int32, (tt, tt), 1)
    before = (ta < tb).astype(BF16)
    prefix = _dot(chosen.astype(BF16), before)
    base = cnt_ref[:, 0:1]
    pos = base + prefix
    cnt_ref[...] = cnt_ref[...] + jnp.sum(chosen, axis=1, keepdims=True)
    eid = lax.broadcasted_iota(jnp.int32, comb.shape, 0).astype(F32)
    for k in range(TOP_K):
        mk = rank == k
        eidx_ref[k:k + 1, :] = jnp.sum(jnp.where(mk, eid, 0.0), axis=0, keepdims=True).astype(jnp.int32)
        pos_ref[k:k + 1, :] = jnp.sum(jnp.where(mk, pos, 0.0), axis=0, keepdims=True).astype(jnp.int32)
        wts_ref[k:k + 1, :] = jnp.sum(jnp.where(mk, comb, 0.0), axis=0, keepdims=True)


def norm_router(x, g3, mod3, wcat_t, whi_t, bias3, layer, sh_idx, sc_idx, rows, n_lat_rows, seq, n_batch):
    d = x.shape[1]
    n_exp = whi_t.shape[1]
    tr = ROW_TILE
    row = functools.partial(_mod_row, n_lat_tiles=n_lat_rows // tr, tiles_per_seq=seq // tr, n_batch=n_batch)
    lists = pl.BlockSpec((TOP_K, tr), lambda i: (0, i))
    return pl.pallas_call(
        _norm_router_kernel,
        grid=(rows // tr,),
        in_specs=[
            pl.BlockSpec((tr, d), lambda i: (i, 0)),
            pl.BlockSpec((None, 1, d), lambda i: (layer, 0, 0)),
            pl.BlockSpec((None, 1, d), lambda i: (row(i) * N_MOD + sh_idx, 0, 0)),
            pl.BlockSpec((None, 1, d), lambda i: (row(i) * N_MOD + sc_idx, 0, 0)),
            pl.BlockSpec((None, 2 * n_exp, d), lambda i: (layer, 0, 0)),
            pl.BlockSpec((None, n_exp, d), lambda i: (layer, 0, 0)),
            pl.BlockSpec((None, n_exp, 1), lambda i: (layer, 0, 0)),
        ],
        out_specs=[
            pl.BlockSpec((tr, d), lambda i: (i, 0)),
            lists, lists, lists,
            pl.BlockSpec((n_exp, V7X_LANES), lambda i: (0, 0)),
        ],
        out_shape=[
            jax.ShapeDtypeStruct((rows, d), F32),
            jax.ShapeDtypeStruct((TOP_K, rows), jnp.int32),
            jax.ShapeDtypeStruct((TOP_K, rows), jnp.int32),
            jax.ShapeDtypeStruct((TOP_K, rows), F32),
            jax.ShapeDtypeStruct((n_exp, V7X_LANES), F32),
        ],
        compiler_params=_params("arbitrary"),
        name="norm_router",
    )(x, g3, mod3, mod3, wcat_t, whi_t, bias3)


def _mm_kernel(a_ref, w_ref, o_ref, wb_ref):
    @pl.when(pl.program_id(1) == 0)
    def _():
        wb_ref[...] = w_ref[...].astype(BF16)

    o_ref[...] = _dot(a_ref[...], wb_ref[...]).astype(o_ref.dtype)


def matmul(a, w, layer, rows, row_off, col_off, ncols, bm, out_dtype=F32):
    k = a.shape[1]
    bn = MM_BN
    ro, co = row_off // bm, col_off // bn
    return pl.pallas_call(
        _mm_kernel,
        grid=(ncols // bn, rows // bm),
        in_specs=[
            pl.BlockSpec((bm, k), lambda j, i: (i + ro, 0)),
            pl.BlockSpec((None, k, bn), lambda j, i: (layer, 0, j + co)),
        ],
        out_specs=pl.BlockSpec((bm, bn), lambda j, i: (i, j)),
        out_shape=jax.ShapeDtypeStruct((rows, ncols), out_dtype),
        scratch_shapes=[pltpu.VMEM((k, bn), BF16)],
        compiler_params=_params("arbitrary", "arbitrary"),
        name="matmul",
    )(a, w)


def _merge_kernel(yc_ref, yr_ref, gc_ref, gr_ref, wc_ref, wr_ref, o_ref, wcb_ref, wrb_ref):
    @pl.when(pl.program_id(1) == 0)
    def _():
        wcb_ref[...] = wc_ref[...].astype(BF16)
        wrb_ref[...] = wr_ref[...].astype(BF16)

    y = jax.nn.sigmoid(gc_ref[...]) * _dot(yc_ref[...], wcb_ref[...])
    y = y + jax.nn.sigmoid(gr_ref[...]) * _dot(yr_ref[...], wrb_ref[...])
    o_ref[...] = y.astype(o_ref.dtype)


def merge_branches(yc, yr, u, w_conv_out, w_ret_out, layer, rows, gate_off, bm):
    kc, kr = yc.shape[1], yr.shape[1]
    d = w_conv_out.shape[2]
    bn = MM_BN
    gco, gro = gate_off // bn, (gate_off + d) // bn
    return pl.pallas_call(
        _merge_kernel,
        grid=(d // bn, rows // bm),
        in_specs=[
            pl.BlockSpec((bm, kc), lambda j, i: (i, 0)),
            pl.BlockSpec((bm, kr), lambda j, i: (i, 0)),
            pl.BlockSpec((bm, bn), lambda j, i: (i, j + gco)),
            pl.BlockSpec((bm, bn), lambda j, i: (i, j + gro)),
            pl.BlockSpec((None, kc, bn), lambda j, i: (layer, 0, j)),
            pl.BlockSpec((None, kr, bn), lambda j, i: (layer, 0, j)),
        ],
        out_specs=pl.BlockSpec((bm, bn), lambda j, i: (i, j)),
        out_shape=jax.ShapeDtypeStruct((rows, d), BF16),
        scratch_shapes=[pltpu.VMEM((kc, bn), BF16), pltpu.VMEM((kr, bn), BF16)],
        compiler_params=_params("arbitrary", "arbitrary"),
        name="merge_branches",
    )(yc, yr, u, u, w_conv_out, w_ret_out)


def _resid_mm_kernel(a_ref, w_ref, x_ref, g_ref, o_ref, wb_ref):
    @pl.when(pl.program_id(1) == 0)
    def _():
        wb_ref[...] = w_ref[...].astype(BF16)

    o_ref[...] = x_ref[...] + g_ref[...] * _dot(a_ref[...], wb_ref[...])


def _resid_mm_add_kernel(a_ref, w_ref, x_ref, g_ref, e_ref, o_ref, wb_ref):
    @pl.when(pl.program_id(1) == 0)
    def _():
        wb_ref[...] = w_ref[...].astype(BF16)

    o_ref[...] = x_ref[...] + g_ref[...] * (e_ref[...] + _dot(a_ref[...], wb_ref[...]))


def resid_matmul(a, w, x, mod3, layer, g_idx, rows, n_lat_rows, seq, n_batch, bm, extra=None):
    k = a.shape[1]
    d = w.shape[2]
    bn = MM_BN
    row = functools.partial(_mod_row, n_lat_tiles=n_lat_rows // bm, tiles_per_seq=seq // bm, n_batch=n_batch)
    tile = pl.BlockSpec((bm, bn), lambda j, i: (i, j))
    in_specs = [
        pl.BlockSpec((bm, k), lambda j, i: (i, 0)),
        pl.BlockSpec((None, k, bn), lambda j, i: (layer, 0, j)),
        tile,
        pl.BlockSpec((None, 1, bn), lambda j, i: (row(i) * N_MOD + g_idx, 0, j)),
    ]
    operands = [a, w, x, mod3]
    if extra is not None:
        in_specs.append(tile)
        operands.append(extra)
    return pl.pallas_call(
        _resid_mm_kernel if extra is None else _resid_mm_add_kernel,
        grid=(d // bn, rows // bm),
        in_specs=in_specs,
        out_specs=tile,
        out_shape=jax.ShapeDtypeStruct((rows, d), F32),
        scratch_shapes=[pltpu.VMEM((k, bn), BF16)],
        compiler_params=_params("arbitrary", "arbitrary"),
        name="resid_matmul",
    )(*operands)


def _conv_kernel(a_ref, g_ref, ap_ref, gp_ref, an_ref, gn_ref, w_ref, b_ref, lng_ref, lnb_ref, o_ref,
                 buf_ref, acc_ref, *, n_lat_tiles, lat_tps, ctx_tps, conv_k):
    i = pl.program_id(0)
    tr, cc = a_ref.shape
    is_lat = i < n_lat_tiles
    pos = jnp.where(is_lat, i % lat_tps, (i - n_lat_tiles) % ctx_tps)
    tps = jnp.where(is_lat, lat_tps, ctx_tps)
    keep_prev = (pos != 0).astype(F32)
    keep_next = (pos != tps - 1).astype(F32)
    buf_ref[HALO:HALO + tr, :] = a_ref[...] * jax.nn.sigmoid(g_ref[...])
    buf_ref[0:HALO, :] = ap_ref[...] * jax.nn.sigmoid(gp_ref[...]) * keep_prev
    buf_ref[HALO + tr:, :] = an_ref[...] * jax.nn.sigmoid(gn_ref[...]) * keep_next
    pad = (conv_k - 1) // 2
    rc = 128
    n_rc = tr // rc

    def chunk(t, carry):
        lanes = pl.ds(pl.multiple_of(t * V7X_LANES, V7X_LANES), V7X_LANES)
        for r in range(n_rc):
            acc = jnp.zeros((rc, V7X_LANES), F32)
            for kk in range(conv_k):
                start = r * rc + HALO - pad + kk
                acc = acc + buf_ref[start:start + rc, lanes] * w_ref[kk:kk + 1, lanes]
            acc_ref[r * rc:(r + 1) * rc, lanes] = acc
        return carry

    lax.fori_loop(0, cc // V7X_LANES, chunk, 0)
    y = acc_ref[...] + b_ref[...]
    mu = jnp.mean(y, axis=-1, keepdims=True)
    yc = y - mu
    var = jnp.mean(yc * yc, axis=-1, keepdims=True)
    z = yc * lax.rsqrt(var + NORM_EPS) * lng_ref[...] + lnb_ref[...]
    o_ref[...] = _silu(z).astype(o_ref.dtype)


def conv_branch(u, conv_w, conv_b3, ln_g3, ln_b3, layer, rows, n_lat_rows, seq, ctx_len):
    conv_k, cc = conv_w.shape[1], conv_w.shape[2]
    tr = ROW_TILE
    hb = tr // HALO
    n_halo_blocks = u.shape[0] // HALO
    kern = functools.partial(_conv_kernel, n_lat_tiles=n_lat_rows // tr, lat_tps=seq // tr,
                             ctx_tps=max(ctx_len // tr, 1), conv_k=conv_k)
    prev = lambda i: jnp.maximum(i * hb - 1, 0)
    nxt = lambda i: jnp.minimum((i + 1) * hb, n_halo_blocks - 1)
    return pl.pallas_call(
        kern,
        grid=(rows // tr,),
        in_specs=[
            pl.BlockSpec((tr, cc), lambda i: (i, 0)),
            pl.BlockSpec((tr, cc), lambda i: (i, 1)),
            pl.BlockSpec((HALO, cc), lambda i: (prev(i), 0)),
            pl.BlockSpec((HALO, cc), lambda i: (prev(i), 1)),
            pl.BlockSpec((HALO, cc), lambda i: (nxt(i), 0)),
            pl.BlockSpec((HALO, cc), lambda i: (nxt(i), 1)),
            pl.BlockSpec((None, conv_k, cc), lambda i: (layer, 0, 0)),
            pl.BlockSpec((None, 1, cc), lambda i: (layer, 0, 0)),
            pl.BlockSpec((None, 1, cc), lambda i: (layer, 0, 0)),
            pl.BlockSpec((None, 1, cc), lambda i: (layer, 0, 0)),
        ],
        out_specs=pl.BlockSpec((tr, cc), lambda i: (i, 0)),
        out_shape=jax.ShapeDtypeStruct((rows, cc), BF16),
        scratch_shapes=[pltpu.VMEM((tr + 2 * HALO, cc), F32), pltpu.VMEM((tr, cc), F32)],
        compiler_params=_params("parallel"),
        name="conv_branch",
    )(u, u, u, u, u, u, conv_w, conv_b3, ln_g3, ln_b3)


def _rope(x, cos, sin):
    half = V7X_LANES
    parts = []
    for p in range(x.shape[1] // half):
        xp = x[:, p * half:(p + 1) * half]
        parts.append(pltpu.roll(xp, half // 2, 1))
    return x * cos + jnp.concatenate(parts, axis=1) * sin


def _decay_terms(lgf, lgb, c):
    ia = lax.broadcasted_iota(jnp.int32, (c, c), 0)
    ib = lax.broadcasted_iota(jnp.int32, (c, c), 1)
    rel = (ia - ib).astype(F32)
    dm = jnp.where(rel >= 0, jnp.exp(lgf * jnp.maximum(rel, 0.0)), 0.0)
    dm = dm + jnp.where(rel <= 0, jnp.exp(lgb * jnp.maximum(-rel, 0.0)), 0.0)
    pos = lax.broadcasted_iota(jnp.int32, (c, 1), 0).astype(F32)
    return dm, pos


def _ret_kernel(lgf_ref, lgb_ref, q_ref, k_ref, v_ref, g_ref, cos_ref, sin_ref, s0f_ref, s0b_ref,
                y_ref, stf_ref, stb_ref, o_scr, q_scr, k_scr, sf_ref, sb_ref, *, layer, n_heads, rope, zero_init):
    h = pl.program_id(1)
    t, dh = q_ref.shape
    c = ROW_TILE
    n = t // c
    lgf = lgf_ref[layer * n_heads + h]
    lgb = lgb_ref[layer * n_heads + h]
    dm, pos = _decay_terms(lgf, lgb, c)
    xi_f = jnp.exp(lgf * (pos + 1.0))
    zeta_f = jnp.exp(lgf * (c - 1.0 - pos))
    xi_b = jnp.exp(lgb * (c - pos))
    zeta_b = jnp.exp(lgb * pos)
    full = jnp.full((1, 1), float(c), F32)
    cd_f = jnp.exp(lgf * full)
    cd_b = jnp.exp(lgb * full)
    scale = dh ** -0.5
    if zero_init:
        sf_ref[...] = jnp.zeros_like(sf_ref)
        sb_ref[...] = jnp.zeros_like(sb_ref)
    else:
        sf_ref[...] = s0f_ref[...]
        sb_ref[...] = s0b_ref[...]

    def fwd(i, carry):
        r = pl.ds(pl.multiple_of(i * c, c), c)
        q = q_ref[r, :]
        k = k_ref[r, :]
        if rope:
            q = _rope(q, cos_ref[r, :], sin_ref[r, :])
            k = _rope(k, cos_ref[r, :], sin_ref[r, :])
        qb = (q * scale).astype(BF16)
        vb = v_ref[r, :].astype(BF16)
        q_scr[r, :] = qb
        k_scr[r, :] = k
        s = _dot_nt(qb, k.astype(BF16)) * dm
        o = _dot(s.astype(BF16), vb) + xi_f * _dot(qb, sf_ref[...].astype(BF16))
        o_scr[r, :] = o
        sf_ref[...] = sf_ref[...] * cd_f + _dot_tn((k * zeta_f).astype(BF16), vb)
        return carry

    lax.fori_loop(0, n, fwd, 0)

    def bwd(j, carry):
        i = n - 1 - j
        r = pl.ds(pl.multiple_of(i * c, c), c)
        qb = q_scr[r, :]
        vb = v_ref[r, :].astype(BF16)
        o = o_scr[r, :] + xi_b * _dot(qb, sb_ref[...].astype(BF16))
        sb_ref[...] = sb_ref[...] * cd_b + _dot_tn((k_scr[r, :] * zeta_b).astype(BF16), vb)
        o = o * lax.rsqrt(jnp.mean(o * o, axis=-1, keepdims=True) + NORM_EPS)
        y_ref[r, :] = (o * _silu(g_ref[r, :])).astype(y_ref.dtype)
        return carry

    lax.fori_loop(0, n, bwd, 0)
    stf_ref[...] = sf_ref[...]
    stb_ref[...] = sb_ref[...]


def retention(lgf, lgb, u, cos, sin, s0f, s0b, layer, n_batch, t, row_off, q_off, n_heads, dh, rope, zero_init):
    rw = n_heads * dh
    ro = row_off // t
    qo, ko, vo, go = ((q_off + m * rw) // dh for m in range(4))
    kern = functools.partial(_ret_kernel, layer=layer, n_heads=n_heads, rope=rope, zero_init=zero_init)
    col = lambda off: pl.BlockSpec((t, dh), lambda b, h, *_: (b + ro, off + h))
    tab = pl.BlockSpec((t, dh), lambda b, h, *_: (0, 0))
    st = pl.BlockSpec((None, None, dh, dh), lambda b, h, *_: (b, h, 0, 0))
    grid_spec = pltpu.PrefetchScalarGridSpec(
        num_scalar_prefetch=2,
        grid=(n_batch, n_heads),
        in_specs=[col(qo), col(ko), col(vo), col(go), tab, tab, st, st],
        out_specs=[pl.BlockSpec((t, dh), lambda b, h, *_: (b, h)), st, st],
        scratch_shapes=[
            pltpu.VMEM((t, dh), F32), pltpu.VMEM((t, dh), BF16), pltpu.VMEM((t, dh), F32),
            pltpu.VMEM((dh, dh), F32), pltpu.VMEM((dh, dh), F32),
        ],
    )
    st_shape = jax.ShapeDtypeStruct((n_batch, n_heads, dh, dh), F32)
    return pl.pallas_call(
        kern,
        grid_spec=grid_spec,
        out_shape=[jax.ShapeDtypeStruct((n_batch * t, rw), BF16), st_shape, st_shape],
        compiler_params=_params("parallel", "parallel"),
        name="retention",
    )(lgf, lgb, u, u, u, u, cos, sin, s0f, s0b)


def _ctx_state_kernel(lgf_ref, lgb_ref, k_ref, v_ref, stf_ref, stb_ref, *, layer, n_heads):
    h = pl.program_id(1)
    t = k_ref.shape[0]
    lgf = lgf_ref[layer * n_heads + h]
    lgb = lgb_ref[layer * n_heads + h]
    pos = lax.broadcasted_iota(jnp.int32, (t, 1), 0).astype(F32)
    k = k_ref[...]
    vb = v_ref[...].astype(BF16)
    stf_ref[...] = _dot_tn((k * jnp.exp(lgf * (t - 1.0 - pos))).astype(BF16), vb)
    stb_ref[...] = _dot_tn((k * jnp.exp(lgb * pos)).astype(BF16), vb)


def ctx_states(lgf, lgb, kv, layer, n_batch, t, n_heads, dh):
    kern = functools.partial(_ctx_state_kernel, layer=layer, n_heads=n_heads)
    st = pl.BlockSpec((None, None, dh, dh), lambda b, h, *_: (b, h, 0, 0))
    grid_spec = pltpu.PrefetchScalarGridSpec(
        num_scalar_prefetch=2,
        grid=(n_batch, n_heads),
        in_specs=[
            pl.BlockSpec((t, dh), lambda b, h, *_: (b, h)),
            pl.BlockSpec((t, dh), lambda b, h, *_: (b, n_heads + h)),
        ],
        out_specs=[st, st],
    )
    st_shape = jax.ShapeDtypeStruct((n_batch, n_heads, dh, dh), F32)
    return pl.pallas_call(
        kern,
        grid_spec=grid_spec,
        out_shape=[st_shape, st_shape],
        compiler_params=_params("parallel", "parallel"),
        name="ctx_states",
    )(lgf, lgb, kv, kv)


def _row_copy(src_ref, src_row, dst_ref, dst_row, sem):
    return pltpu.make_async_copy(src_ref.at[pl.ds(src_row, 1), :], dst_ref.at[pl.ds(dst_row, 1), :], sem)


def _rows_done(like_ref, n_rows, sem):
    span = like_ref.at[pl.ds(0, n_rows), :]
    return pltpu.make_async_copy(span, span, sem)


def _dispatch_kernel(slot_ref, ztile_ref, h_ref, xs_ref, zero_ref, sem_ref, zsem_ref, *, n_rows, n_exp, bmx):
    i = pl.program_id(0)
    tt = h_ref.shape[0]

    n_tiles = xs_ref.shape[0] // bmx

    def zero_copy(tile):
        start = pl.multiple_of(tile * bmx, bmx)
        return pltpu.make_async_copy(zero_ref, xs_ref.at[pl.ds(start, bmx), :], zsem_ref)

    @pl.when(i == 0)
    def _():
        zero_ref[...] = jnp.zeros_like(zero_ref)
        n_used = ztile_ref[n_exp]

        def start(e, carry):
            @pl.when(ztile_ref[e] >= 0)
            def _():
                zero_copy(ztile_ref[e]).start()
            return carry

        def wait(e, carry):
            @pl.when(ztile_ref[e] >= 0)
            def _():
                zero_copy(ztile_ref[e]).wait()
            return carry

        def start_tail(j, carry):
            zero_copy(j).start()
            return carry

        def wait_tail(j, carry):
            zero_copy(j).wait()
            return carry

        lax.fori_loop(0, n_exp, start, 0)
        lax.fori_loop(n_used, n_tiles, start_tail, 0)
        lax.fori_loop(0, n_exp, wait, 0)
        lax.fori_loop(n_used, n_tiles, wait_tail, 0)

    def issue(t, carry):
        for k in range(TOP_K):
            _row_copy(h_ref, t, xs_ref, slot_ref[k * n_rows + i * tt + t], sem_ref).start()
        return carry

    lax.fori_loop(0, tt, issue, 0)
    _rows_done(xs_ref, TOP_K * tt, sem_ref).wait()


def moe_dispatch(slots, ztile, h, n_slots, bmx):
    rows, d = h.shape
    n_exp = ztile.shape[0] - 1
    tt = ROW_TILE
    kern = functools.partial(_dispatch_kernel, n_rows=rows, n_exp=n_exp, bmx=bmx)
    grid_spec = pltpu.PrefetchScalarGridSpec(
        num_scalar_prefetch=2,
        grid=(rows // tt,),
        in_specs=[pl.BlockSpec((tt, d), lambda i, *_: (i, 0))],
        out_specs=pl.BlockSpec(memory_space=pl.ANY),
        scratch_shapes=[pltpu.VMEM((bmx, d), F32), pltpu.SemaphoreType.DMA(()), pltpu.SemaphoreType.DMA(())],
    )
    return pl.pallas_call(
        kern,
        grid_spec=grid_spec,
        out_shape=jax.ShapeDtypeStruct((n_slots, d), F32),
        compiler_params=_params("arbitrary"),
        name="moe_dispatch",
    )(slots, ztile, h)


def _experts_kernel(texp_ref, nused_ref, x_ref, w1_ref, w3_ref, w2_ref, o_ref, w1b_ref, w3b_ref, w2b_ref):
    i = pl.program_id(0)
    valid = i < nused_ref[0]
    fresh = (i == 0) | (valid & (texp_ref[i] != texp_ref[jnp.maximum(i - 1, 0)]))

    @pl.when(fresh)
    def _():
        w1b_ref[...] = w1_ref[...].astype(BF16)
        w3b_ref[...] = w3_ref[...].astype(BF16)
        w2b_ref[...] = w2_ref[...].astype(BF16)

    @pl.when(valid)
    def _():
        x = x_ref[...].astype(BF16)
        a = _silu(_dot(x, w1b_ref[...])) * _dot(x, w3b_ref[...])
        o_ref[...] = _dot(a.astype(BF16), w2b_ref[...])

    @pl.when(jnp.logical_not(valid))
    def _():
        o_ref[...] = jnp.zeros_like(o_ref)


def moe_experts(tile_expert, n_used, xs, w1, w3, w2, layer, bmx):
    n_slots, d = xs.shape
    hid = w1.shape[3]
    tile = lambda i, texp, nused: jnp.minimum(i, nused[0] - 1)
    rows = pl.BlockSpec((bmx, d), lambda i, texp, nused: (tile(i, texp, nused), 0))
    out_rows = pl.BlockSpec((bmx, d), lambda i, texp, nused: (i, 0))
    up = pl.BlockSpec((None, None, d, hid), lambda i, texp, nused: (layer, texp[tile(i, texp, nused)], 0, 0))
    down = pl.BlockSpec((None, None, hid, d), lambda i, texp, nused: (layer, texp[tile(i, texp, nused)], 0, 0))
    grid_spec = pltpu.PrefetchScalarGridSpec(
        num_scalar_prefetch=2,
        grid=(n_slots // bmx,),
        in_specs=[rows, up, up, down],
        out_specs=out_rows,
        scratch_shapes=[pltpu.VMEM((d, hid), BF16), pltpu.VMEM((d, hid), BF16), pltpu.VMEM((hid, d), BF16)],
    )
    return pl.pallas_call(
        _experts_kernel,
        grid_spec=grid_spec,
        out_shape=jax.ShapeDtypeStruct((n_slots, d), F32),
        compiler_params=_params("arbitrary"),
        name="moe_experts",
    )(tile_expert, n_used, xs, w1, w3, w2)


def _combine_kernel(slot_ref, w_ref, ys_ref, o_ref, buf_ref, sem_ref, *, n_rows):
    i = pl.program_id(0)
    n = pl.num_programs(0)
    tt = o_ref.shape[0]

    def issue(tile, buf):
        def body(t, carry):
            for k in range(TOP_K):
                _row_copy(ys_ref, slot_ref[k * n_rows + tile * tt + t], buf_ref.at[buf], k * tt + t,
                          sem_ref.at[buf]).start()
            return carry

        lax.fori_loop(0, tt, body, 0)

    @pl.when(i == 0)
    def _():
        issue(0, 0)

    @pl.when(i + 1 < n)
    def _():
        issue(i + 1, (i + 1) % 2)

    buf = i % 2
    _rows_done(ys_ref, TOP_K * tt, sem_ref.at[buf]).wait()
    w = w_ref[...]
    acc = buf_ref[buf, 0:tt, :] * w[:, 0:1]
    for k in range(1, TOP_K):
        acc = acc + buf_ref[buf, k * tt:(k + 1) * tt, :] * w[:, k:k + 1]
    o_ref[...] = acc


def moe_combine(slots, wts, ys, rows):
    d = ys.shape[1]
    tt = COMBINE_TT
    kern = functools.partial(_combine_kernel, n_rows=rows)
    grid_spec = pltpu.PrefetchScalarGridSpec(
        num_scalar_prefetch=1,
        grid=(rows // tt,),
        in_specs=[pl.BlockSpec((tt, TOP_K), lambda i, *_: (i, 0)), pl.BlockSpec(memory_space=pl.ANY)],
        out_specs=pl.BlockSpec((tt, d), lambda i, *_: (i, 0)),
        scratch_shapes=[pltpu.VMEM((2, TOP_K * tt, d), F32), pltpu.SemaphoreType.DMA((2,))],
    )
    return pl.pallas_call(
        kern,
        grid_spec=grid_spec,
        out_shape=jax.ShapeDtypeStruct((rows, d), F32),
        compiler_params=_params("arbitrary"),
        name="moe_combine",
    )(slots, wts, ys)


def routing_tables(eidx, pos, counts, bmx, n_tiles):
    n_exp = counts.shape[0]
    padded = (counts + bmx - 1) // bmx * bmx
    ends = jnp.cumsum(padded)
    starts = ends - padded
    slots = (jnp.take(starts, eidx) + pos).reshape(-1)
    tile_ends = ends // bmx
    n_used = tile_ends[-1:]
    tile_expert = jnp.searchsorted(tile_ends, jnp.arange(n_tiles, dtype=jnp.int32), side="right")
    tile_expert = jnp.minimum(tile_expert, n_exp - 1).astype(jnp.int32)
    ztile = jnp.concatenate([jnp.where(counts > 0, tile_ends - 1, -1), n_used]).astype(jnp.int32)
    return slots.astype(jnp.int32), ztile, tile_expert, n_used.astype(jnp.int32)


def _glu_mm_kernel(h_ref, w1_ref, w3_ref, o_ref, w1b_ref, w3b_ref):
    @pl.when(pl.program_id(1) == 0)
    def _():
        w1b_ref[...] = w1_ref[...].astype(BF16)
        w3b_ref[...] = w3_ref[...].astype(BF16)

    h = h_ref[...].astype(BF16)
    o_ref[...] = (_silu(_dot(h, w1b_ref[...])) * _dot(h, w3b_ref[...])).astype(o_ref.dtype)


def glu_matmul(h, w1, w3, layer, rows, bm):
    k = h.shape[1]
    n = w1.shape[2]
    bn = HID_CHUNK
    return pl.pallas_call(
        _glu_mm_kernel,
        grid=(n // bn, rows // bm),
        in_specs=[
            pl.BlockSpec((bm, k), lambda j, i: (i, 0)),
            pl.BlockSpec((None, k, bn), lambda j, i: (layer, 0, j)),
            pl.BlockSpec((None, k, bn), lambda j, i: (layer, 0, j)),
        ],
        out_specs=pl.BlockSpec((bm, bn), lambda j, i: (i, j)),
        out_shape=jax.ShapeDtypeStruct((rows, n), BF16),
        scratch_shapes=[pltpu.VMEM((k, bn), BF16), pltpu.VMEM((k, bn), BF16)],
        compiler_params=_params("arbitrary", "arbitrary"),
        name="glu_matmul",
    )(h, w1, w3)


def _rope_tables(seq, dh):
    rows = seq // GRID_W
    row = jnp.repeat(jnp.arange(rows, dtype=F32), GRID_W)
    col = jnp.tile(jnp.arange(GRID_W, dtype=F32), rows)
    quarter = dh // 4
    inv_freq = ROPE_BASE ** (-jnp.arange(quarter, dtype=F32) / quarter)
    ang_r = row[:, None] * inv_freq[None, :]
    ang_c = col[:, None] * inv_freq[None, :]
    cos = jnp.concatenate([jnp.cos(ang_r), jnp.cos(ang_r), jnp.cos(ang_c), jnp.cos(ang_c)], axis=-1)
    sin = jnp.concatenate([-jnp.sin(ang_r), jnp.sin(ang_r), -jnp.sin(ang_c), jnp.sin(ang_c)], axis=-1)
    return cos, sin


def kernel(x, c, ctx, c_ctx, mod_w, mod_b, norm1_g, norm2_g, w_in, conv_w, conv_b, conv_ln_g, conv_ln_b,
           w_conv_out, ret_log_gamma_fwd, ret_log_gamma_bwd, w_ret_out, w_merge_out, router_w, router_bias,
           exp_w1, exp_w3, exp_w2, shared_w1, shared_w3, shared_w2, final_g):
    b, s, d = x.shape
    n_ctx = ctx.shape[1]
    depth = mod_w.shape[0]
    cc = conv_w.shape[2]
    n_heads = ret_log_gamma_fwd.shape[1]
    rw = w_ret_out.shape[1]
    dh = rw // n_heads
    n_exp = router_w.shape[2]
    n_lat, n_cx = b * s, b * n_ctx
    n_all = n_lat + n_cx
    bm = min(MM_BM, s, n_cx)
    assert b + 1 <= 8 and s % ROW_TILE == 0 and n_ctx % ROW_TILE == 0 and dh == 2 * V7X_LANES
    assert s % bm == 0 and n_cx % bm == 0 and s % MOE_TM == 0 and n_cx % MOE_TM == 0
    q_off, gate_off = 2 * cc, 2 * cc + 4 * rw

    xa = jnp.concatenate([x.reshape(n_lat, d), ctx.reshape(n_cx, d)], axis=0)
    c8 = jnp.zeros((8, d), F32).at[:b].set(c).at[b].set(c_ctx)
    mod_b3 = mod_b.reshape(depth, 1, -1)
    n1g, n2g = norm1_g.reshape(depth, 1, d), norm2_g.reshape(depth, 1, d)
    conv_b3, ln_g3, ln_b3 = (a.reshape(depth, 1, cc) for a in (conv_b, conv_ln_g, conv_ln_b))
    lgf, lgb = ret_log_gamma_fwd.reshape(-1), ret_log_gamma_bwd.reshape(-1)
    rw_t = jnp.swapaxes(router_w, 1, 2)
    rw_hi = rw_t.astype(BF16)
    rw_lo = (rw_t - rw_hi.astype(F32)).astype(BF16)
    rw_cat = jnp.concatenate([rw_hi, rw_lo], axis=1)
    bias3 = router_bias.reshape(depth, n_exp, 1)
    cos, sin = _rope_tables(s, dh)
    zero_st = jnp.zeros((b, n_heads, dh, dh), F32)

    for layer in range(depth):
        last = layer == depth - 1
        rows = n_lat if last else n_all
        seq_args = dict(n_lat_rows=n_lat, seq=s, n_batch=b)
        mod3 = modulation(c8, mod_w, mod_b3, layer).reshape(8 * N_MOD, 1, d)
        h1 = norm_mod(xa, n1g, mod3, layer, 0, 1, n_all, **seq_args)
        if last:
            u = matmul(h1, w_in, layer, n_lat, 0, 0, w_in.shape[2], bm)
            kv_c = matmul(h1, w_in, layer, n_cx, n_lat, q_off + rw, 2 * rw, bm)
            st_f, st_b = ctx_states(lgf, lgb, kv_c, layer, b, n_ctx, n_heads, dh)
        else:
            u = matmul(h1, w_in, layer, n_all, 0, 0, w_in.shape[2], bm)
            yr_c, st_f, st_b = retention(lgf, lgb, u, cos, sin, zero_st, zero_st, layer, b, n_ctx, n_lat, q_off,
                                         n_heads, dh, rope=False, zero_init=True)
        yc = conv_branch(u, conv_w, conv_b3, ln_g3, ln_b3, layer, rows, n_lat, s, n_ctx)
        yr, _, _ = retention(lgf, lgb, u, cos, sin, st_f, st_b, layer, b, s, 0, q_off, n_heads, dh,
                             rope=True, zero_init=False)
        if not last:
            yr = jnp.concatenate([yr, yr_c], axis=0)
        mixed = merge_branches(yc, yr, u, w_conv_out, w_ret_out, layer, rows, gate_off, bm)
        xa = resid_matmul(mixed, w_merge_out, xa, mod3, layer, 2, rows, bm=bm, **seq_args)
        h2, eidx, pos, wts, cnt = norm_router(xa, n2g, mod3, rw_cat, rw_hi, bias3, layer, 3, 4, rows, **seq_args)
        n_tiles = TOP_K * rows // MOE_BMX + n_exp
        slots, ztile, tile_expert, n_used = routing_tables(eidx, pos, cnt[:, 0].astype(jnp.int32), MOE_BMX, n_tiles)
        xs = moe_dispatch(slots, ztile, h2, n_tiles * MOE_BMX, MOE_BMX)
        ys = moe_experts(tile_expert, n_used, xs, exp_w1, exp_w3, exp_w2, layer, MOE_BMX)
        routed = moe_combine(slots, wts.T, ys, rows)
        act = glu_matmul(h2, shared_w1, shared_w3, layer, rows, MOE_TM)
        xa = resid_matmul(act, shared_w2, xa, mod3, layer, 5, rows, bm=bm, extra=routed, **seq_args)
    out = final_norm(xa, final_g.reshape(1, d), n_lat)
    return out.reshape(b, s, d)
```

```python
import functools

import jax
import jax.numpy as jnp
from jax import lax
from jax.experimental import pallas as pl
from jax.experimental.pallas import tpu as pltpu

GRID_W = 64
N_GROUPS = 8
TOPK_GROUPS = 4
TOP_K = 8
ROUTED_SCALE = 2.5
N_MOD = 6
NORM_EPS = 1e-6
ROPE_BASE = 10000.0

V7X_LANES = 128
V7X_SUBLANES = 8
V7X_VMEM_BYTES = 64 * 1024 * 1024
VMEM_LIMIT = V7X_VMEM_BYTES - 8 * 1024 * 1024

ROW_TILE = 256
HALO = 16
MM_BM = 1024
MM_BN = 512
MOE_TM = 512
MOE_BMX = 256
COMBINE_TT = 64
HID_CHUNK = 256

BF16 = jnp.bfloat16
F32 = jnp.float32


def _params(*sem):
    return pltpu.CompilerParams(dimension_semantics=sem, vmem_limit_bytes=VMEM_LIMIT)


def _dot(a, b):
    return jnp.dot(a, b, preferred_element_type=F32)


def _dot_nt(a, b):
    return lax.dot_general(a, b, (((1,), (1,)), ((), ())), preferred_element_type=F32)


def _dot_tn(a, b):
    return lax.dot_general(a, b, (((0,), (0,)), ((), ())), preferred_element_type=F32)


def _silu(x):
    return x * jax.nn.sigmoid(x)


def _pack_halves(v):
    half = v.shape[1] // 2
    lo = pltpu.bitcast(v[:, :half].astype(BF16).astype(F32), jnp.uint32)
    hi = pltpu.bitcast(v[:, half:].astype(BF16).astype(F32), jnp.uint32)
    return (hi & jnp.uint32(0xFFFF0000)) | (lo >> 16)


def _unpack_halves(p):
    lo = pltpu.bitcast(p << 16, F32)
    hi = pltpu.bitcast(p & jnp.uint32(0xFFFF0000), F32)
    return lo, hi


def _unpack_bf16(p):
    lo, hi = _unpack_halves(p)
    return jnp.concatenate([lo.astype(BF16), hi.astype(BF16)], axis=1)


def _mod_row(i, n_lat_tiles, tiles_per_seq, n_batch):
    return jnp.where(i < n_lat_tiles, i // tiles_per_seq, n_batch)


def _mod_kernel(c_ref, w_ref, b_ref, o_ref):
    c = c_ref[...]
    sc = _silu(c).astype(BF16)
    o_ref[...] = _dot(sc, w_ref[...].astype(BF16)) + b_ref[...]


def modulation(c8, mod_w, mod_b3, layer):
    d = c8.shape[1]
    n = mod_w.shape[2]
    bn = MM_BN
    return pl.pallas_call(
        _mod_kernel,
        grid=(n // bn,),
        in_specs=[
            pl.BlockSpec((8, d), lambda j: (0, 0)),
            pl.BlockSpec((None, d, bn), lambda j: (layer, 0, j)),
            pl.BlockSpec((None, 1, bn), lambda j: (layer, 0, j)),
        ],
        out_specs=pl.BlockSpec((8, bn), lambda j: (0, j)),
        out_shape=jax.ShapeDtypeStruct((8, n), F32),
        compiler_params=_params("arbitrary"),
        name="modulation",
    )(c8, mod_w, mod_b3)


def _rms(x, g):
    return x * lax.rsqrt(jnp.mean(x * x, axis=-1, keepdims=True) + NORM_EPS) * g


def _norm_mod_kernel(x_ref, g_ref, sh_ref, sc_ref, o_ref):
    h = _rms(x_ref[...], g_ref[...]) * (1.0 + sc_ref[...]) + sh_ref[...]
    o_ref[...] = h.astype(o_ref.dtype)


def norm_mod(x, g3, mod3, layer, sh_idx, sc_idx, rows, n_lat_rows, seq, n_batch):
    d = x.shape[1]
    tr = ROW_TILE
    row = functools.partial(_mod_row, n_lat_tiles=n_lat_rows // tr, tiles_per_seq=seq // tr, n_batch=n_batch)
    return pl.pallas_call(
        _norm_mod_kernel,
        grid=(rows // tr,),
        in_specs=[
            pl.BlockSpec((tr, d), lambda i: (i, 0)),
            pl.BlockSpec((None, 1, d), lambda i: (layer, 0, 0)),
            pl.BlockSpec((None, 1, d), lambda i: (row(i) * N_MOD + sh_idx, 0, 0)),
            pl.BlockSpec((None, 1, d), lambda i: (row(i) * N_MOD + sc_idx, 0, 0)),
        ],
        out_specs=pl.BlockSpec((tr, d), lambda i: (i, 0)),
        out_shape=jax.ShapeDtypeStruct((rows, d), BF16),
        compiler_params=_params("parallel"),
        name="norm_mod",
    )(x, g3, mod3, mod3)


def _final_norm_kernel(x_ref, g_ref, o_ref):
    o_ref[...] = _rms(x_ref[...], g_ref[...])


def final_norm(x, g2, rows):
    d = x.shape[1]
    tr = ROW_TILE
    return pl.pallas_call(
        _final_norm_kernel,
        grid=(rows // tr,),
        in_specs=[pl.BlockSpec((tr, d), lambda i: (i, 0)), pl.BlockSpec((1, d), lambda i: (0, 0))],
        out_specs=pl.BlockSpec((tr, d), lambda i: (i, 0)),
        out_shape=jax.ShapeDtypeStruct((rows, d), F32),
        compiler_params=_params("parallel"),
        name="final_norm",
    )(x, g2)


def _route(logits_t, bias):
    n_exp, tt = logits_t.shape
    per = n_exp // N_GROUPS
    s = jax.nn.sigmoid(logits_t)
    biased = s + bias
    g3 = biased.reshape(N_GROUPS, per, tt)
    mem = lax.broadcasted_iota(jnp.int32, g3.shape, 1)
    m1 = jnp.max(g3, axis=1, keepdims=True)
    first = jnp.min(jnp.where(g3 == m1, mem, per), axis=1, keepdims=True)
    m2 = jnp.max(jnp.where(mem == first, -jnp.inf, g3), axis=1, keepdims=True)
    gs = m1 + m2
    gid = lax.broadcasted_iota(jnp.int32, gs.shape, 0)
    grank = jnp.zeros(gs.shape, jnp.int32)
    for j in range(N_GROUPS):
        other = gs[j:j + 1]
        ahead = (other > gs) | ((other == gs) & (gid > j))
        grank = grank + ahead.astype(jnp.int32)
    gmask = grank < TOPK_GROUPS
    masked = jnp.where(gmask, g3, -jnp.inf).reshape(n_exp, tt)
    eid = lax.broadcasted_iota(jnp.int32, masked.shape, 0)
    rank = jnp.zeros(masked.shape, jnp.int32)
    for j in range(n_exp):
        other = masked[j:j + 1, :]
        ahead = (other > masked) | ((other == masked) & (eid > j))
        rank = rank + ahead.astype(jnp.int32)
    sel = jnp.where(rank < TOP_K, s, 0.0)
    return sel / jnp.sum(sel, axis=0, keepdims=True) * ROUTED_SCALE, rank


def _norm_router_kernel(x_ref, g_ref, sh_ref, sc_ref, wcat_ref, whi_ref, bias_ref,
                        h_ref, eidx_ref, pos_ref, wts_ref, cnt_ref):
    i = pl.program_id(0)
    h = _rms(x_ref[...], g_ref[...]) * (1.0 + sc_ref[...]) + sh_ref[...]
    h_hi = h.astype(BF16)
    h_lo = (h - h_hi.astype(F32)).astype(BF16)
    h_ref[...] = _pack_halves(h)
    n_exp = whi_ref.shape[0]
    lt = _dot_nt(wcat_ref[...], h_hi)
    logits_t = lt[:n_exp] + lt[n_exp:] + _dot_nt(whi_ref[...], h_lo)
    comb, rank = _route(logits_t, bias_ref[...])
    tt = comb.shape[1]

    @pl.when(i == 0)
    def _():
        cnt_ref[...] = jnp.zeros_like(cnt_ref)

    chosen = (rank < TOP_K).astype(F32)
    ta = lax.broadcasted_iota(jnp.int32, (tt, tt), 0)
    tb = lax.broadcasted_iota(jnp.int32, (tt, tt), 1)
    before = (ta < tb).astype(BF16)
    prefix = _dot(chosen.astype(BF16), before)
    base = cnt_ref[:, 0:1]
    pos = base + prefix
    cnt_ref[...] = cnt_ref[...] + jnp.sum(chosen, axis=1, keepdims=True)
    eid = lax.broadcasted_iota(jnp.int32, comb.shape, 0).astype(F32)
    for k in range(TOP_K):
        mk = rank == k
        eidx_ref[k:k + 1, :] = jnp.sum(jnp.where(mk, eid, 0.0), axis=0, keepdims=True).astype(jnp.int32)
        pos_ref[k:k + 1, :] = jnp.sum(jnp.where(mk, pos, 0.0), axis=0, keepdims=True).astype(jnp.int32)
        wts_ref[k:k + 1, :] = jnp.sum(jnp.where(mk, comb, 0.0), axis=0, keepdims=True)


def norm_router(x, g3, mod3, wcat_t, whi_t, bias3, layer, sh_idx, sc_idx, rows, n_lat_rows, seq, n_batch):
    d = x.shape[1]
    n_exp = whi_t.shape[1]
    tr = ROW_TILE
    row = functools.partial(_mod_row, n_lat_tiles=n_lat_rows // tr, tiles_per_seq=seq // tr, n_batch=n_batch)
    lists = pl.BlockSpec((TOP_K, tr), lambda i: (0, i))
    return pl.pallas_call(
        _norm_router_kernel,
        grid=(rows // tr,),
        in_specs=[
            pl.BlockSpec((tr, d), lambda i: (i, 0)),
            pl.BlockSpec((None, 1, d), lambda i: (layer, 0, 0)),
            pl.BlockSpec((None, 1, d), lambda i: (row(i) * N_MOD + sh_idx, 0, 0)),
            pl.BlockSpec((None, 1, d), lambda i: (row(i) * N_MOD + sc_idx, 0, 0)),
            pl.BlockSpec((None, 2 * n_exp, d), lambda i: (layer, 0, 0)),
            pl.BlockSpec((None, n_exp, d), lambda i: (layer, 0, 0)),
            pl.BlockSpec((None, n_exp, 1), lambda i: (layer, 0, 0)),
        ],
        out_specs=[
            pl.BlockSpec((tr, d // 2), lambda i: (i, 0)),
            lists, lists, lists,
            pl.BlockSpec((n_exp, V7X_LANES), lambda i: (0, 0)),
        ],
        out_shape=[
            jax.ShapeDtypeStruct((rows, d // 2), jnp.uint32),
            jax.ShapeDtypeStruct((TOP_K, rows), jnp.int32),
            jax.ShapeDtypeStruct((TOP_K, rows), jnp.int32),
            jax.ShapeDtypeStruct((TOP_K, rows), F32),
            jax.ShapeDtypeStruct((n_exp, V7X_LANES), F32),
        ],
        compiler_params=_params("arbitrary"),
        name="norm_router",
    )(x, g3, mod3, mod3, wcat_t, whi_t, bias3)


def _mm_kernel(a_ref, w_ref, o_ref, wb_ref):
    @pl.when(pl.program_id(1) == 0)
    def _():
        wb_ref[...] = w_ref[...].astype(BF16)

    o_ref[...] = _dot(a_ref[...], wb_ref[...]).astype(o_ref.dtype)


def matmul(a, w, layer, rows, row_off, col_off, ncols, bm, out_dtype=F32):
    k = a.shape[1]
    bn = MM_BN
    ro, co = row_off // bm, col_off // bn
    return pl.pallas_call(
        _mm_kernel,
        grid=(ncols // bn, rows // bm),
        in_specs=[
            pl.BlockSpec((bm, k), lambda j, i: (i + ro, 0)),
            pl.BlockSpec((None, k, bn), lambda j, i: (layer, 0, j + co)),
        ],
        out_specs=pl.BlockSpec((bm, bn), lambda j, i: (i, j)),
        out_shape=jax.ShapeDtypeStruct((rows, ncols), out_dtype),
        scratch_shapes=[pltpu.VMEM((k, bn), BF16)],
        compiler_params=_params("arbitrary", "arbitrary"),
        name="matmul",
    )(a, w)


def _merge_kernel(yc_ref, yr_ref, gc_ref, gr_ref, wc_ref, wr_ref, o_ref, wcb_ref, wrb_ref):
    @pl.when(pl.program_id(1) == 0)
    def _():
        wcb_ref[...] = wc_ref[...].astype(BF16)
        wrb_ref[...] = wr_ref[...].astype(BF16)

    y = jax.nn.sigmoid(gc_ref[...]) * _dot(yc_ref[...], wcb_ref[...])
    y = y + jax.nn.sigmoid(gr_ref[...]) * _dot(yr_ref[...], wrb_ref[...])
    o_ref[...] = y.astype(o_ref.dtype)


def merge_branches(yc, yr, u, w_conv_out, w_ret_out, layer, rows, gate_off, bm):
    kc, kr = yc.shape[1], yr.shape[1]
    d = w_conv_out.shape[2]
    bn = MM_BN
    gco, gro = gate_off // bn, (gate_off + d) // bn
    return pl.pallas_call(
        _merge_kernel,
        grid=(d // bn, rows // bm),
        in_specs=[
            pl.BlockSpec((bm, kc), lambda j, i: (i, 0)),
            pl.BlockSpec((bm, kr), lambda j, i: (i, 0)),
            pl.BlockSpec((bm, bn), lambda j, i: (i, j + gco)),
            pl.BlockSpec((bm, bn), lambda j, i: (i, j + gro)),
            pl.BlockSpec((None, kc, bn), lambda j, i: (layer, 0, j)),
            pl.BlockSpec((None, kr, bn), lambda j, i: (layer, 0, j)),
        ],
        out_specs=pl.BlockSpec((bm, bn), lambda j, i: (i, j)),
        out_shape=jax.ShapeDtypeStruct((rows, d), BF16),
        scratch_shapes=[pltpu.VMEM((kc, bn), BF16), pltpu.VMEM((kr, bn), BF16)],
        compiler_params=_params("arbitrary", "arbitrary"),
        name="merge_branches",
    )(yc, yr, u, u, w_conv_out, w_ret_out)


def _resid_mm_kernel(a_ref, w_ref, x_ref, g_ref, o_ref, wb_ref):
    @pl.when(pl.program_id(1) == 0)
    def _():
        wb_ref[...] = w_ref[...].astype(BF16)

    o_ref[...] = x_ref[...] + g_ref[...] * _dot(a_ref[...], wb_ref[...])


def _resid_mm_add_kernel(a_ref, w_ref, x_ref, g_ref, e_ref, o_ref, wb_ref):
    @pl.when(pl.program_id(1) == 0)
    def _():
        wb_ref[...] = w_ref[...].astype(BF16)

    o_ref[...] = x_ref[...] + g_ref[...] * (e_ref[...] + _dot(a_ref[...], wb_ref[...]))


def resid_matmul(a, w, x, mod3, layer, g_idx, rows, n_lat_rows, seq, n_batch, bm, extra=None):
    k = a.shape[1]
    d = w.shape[2]
    bn = MM_BN
    row = functools.partial(_mod_row, n_lat_tiles=n_lat_rows // bm, tiles_per_seq=seq // bm, n_batch=n_batch)
    tile = pl.BlockSpec((bm, bn), lambda j, i: (i, j))
    in_specs = [
        pl.BlockSpec((bm, k), lambda j, i: (i, 0)),
        pl.BlockSpec((None, k, bn), lambda j, i: (layer, 0, j)),
        tile,
        pl.BlockSpec((None, 1, bn), lambda j, i: (row(i) * N_MOD + g_idx, 0, j)),
    ]
    operands = [a, w, x, mod3]
    if extra is not None:
        in_specs.append(tile)
        operands.append(extra)
    return pl.pallas_call(
        _resid_mm_kernel if extra is None else _resid_mm_add_kernel,
        grid=(d // bn, rows // bm),
        in_specs=in_specs,
        out_specs=tile,
        out_shape=jax.ShapeDtypeStruct((rows, d), F32),
        scratch_shapes=[pltpu.VMEM((k, bn), BF16)],
        compiler_params=_params("arbitrary", "arbitrary"),
        name="resid_matmul",
    )(*operands)


def _conv_kernel(a_ref, g_ref, ap_ref, gp_ref, an_ref, gn_ref, w_ref, b_ref, lng_ref, lnb_ref, o_ref,
                 buf_ref, acc_ref, shift_ref, *, n_lat_tiles, lat_tps, ctx_tps, conv_k):
    i = pl.program_id(0)
    tr, cc = a_ref.shape
    is_lat = i < n_lat_tiles
    pos = jnp.where(is_lat, i % lat_tps, (i - n_lat_tiles) % ctx_tps)
    tps = jnp.where(is_lat, lat_tps, ctx_tps)
    keep_prev = (pos != 0).astype(F32)
    keep_next = (pos != tps - 1).astype(F32)
    buf_ref[HALO:HALO + tr, :] = a_ref[...] * jax.nn.sigmoid(g_ref[...])
    buf_ref[0:HALO, :] = ap_ref[...] * jax.nn.sigmoid(gp_ref[...]) * keep_prev
    buf_ref[HALO + tr:, :] = an_ref[...] * jax.nn.sigmoid(gn_ref[...]) * keep_next
    pad = (conv_k - 1) // 2
    rc = 128
    n_rc = tr // rc
    span = shift_ref.shape[1]

    def chunk(t, carry):
        lanes = pl.ds(pl.multiple_of(t * V7X_LANES, V7X_LANES), V7X_LANES)
        for s in range(1, V7X_SUBLANES):
            shift_ref[s - 1] = buf_ref[s:s + span, lanes]
        for r in range(n_rc):
            acc = jnp.zeros((rc, V7X_LANES), F32)
            for kk in range(conv_k):
                whole, s = divmod(HALO - pad + kk, V7X_SUBLANES)
                start = r * rc + whole * V7X_SUBLANES
                if s == 0:
                    window = buf_ref[start:start + rc, lanes]
                else:
                    window = shift_ref[s - 1, start:start + rc, :]
                acc = acc + window * w_ref[kk:kk + 1, lanes]
            acc_ref[r * rc:(r + 1) * rc, lanes] = acc
        return carry

    lax.fori_loop(0, cc // V7X_LANES, chunk, 0)
    y = acc_ref[...] + b_ref[...]
    mu = jnp.mean(y, axis=-1, keepdims=True)
    yc = y - mu
    var = jnp.mean(yc * yc, axis=-1, keepdims=True)
    z = yc * lax.rsqrt(var + NORM_EPS) * lng_ref[...] + lnb_ref[...]
    o_ref[...] = _silu(z).astype(o_ref.dtype)


def conv_branch(u, conv_w, conv_b3, ln_g3, ln_b3, layer, rows, n_lat_rows, seq, ctx_len):
    conv_k, cc = conv_w.shape[1], conv_w.shape[2]
    tr = ROW_TILE
    hb = tr // HALO
    n_halo_blocks = u.shape[0] // HALO
    kern = functools.partial(_conv_kernel, n_lat_tiles=n_lat_rows // tr, lat_tps=seq // tr,
                             ctx_tps=max(ctx_len // tr, 1), conv_k=conv_k)
    prev = lambda i: jnp.maximum(i * hb - 1, 0)
    nxt = lambda i: jnp.minimum((i + 1) * hb, n_halo_blocks - 1)
    return pl.pallas_call(
        kern,
        grid=(rows // tr,),
        in_specs=[
            pl.BlockSpec((tr, cc), lambda i: (i, 0)),
            pl.BlockSpec((tr, cc), lambda i: (i, 1)),
            pl.BlockSpec((HALO, cc), lambda i: (prev(i), 0)),
            pl.BlockSpec((HALO, cc), lambda i: (prev(i), 1)),
            pl.BlockSpec((HALO, cc), lambda i: (nxt(i), 0)),
            pl.BlockSpec((HALO, cc), lambda i: (nxt(i), 1)),
            pl.BlockSpec((None, conv_k, cc), lambda i: (layer, 0, 0)),
            pl.BlockSpec((None, 1, cc), lambda i: (layer, 0, 0)),
            pl.BlockSpec((None, 1, cc), lambda i: (layer, 0, 0)),
            pl.BlockSpec((None, 1, cc), lambda i: (layer, 0, 0)),
        ],
        out_specs=pl.BlockSpec((tr, cc), lambda i: (i, 0)),
        out_shape=jax.ShapeDtypeStruct((rows, cc), BF16),
        scratch_shapes=[
            pltpu.VMEM((tr + 2 * HALO, cc), F32),
            pltpu.VMEM((tr, cc), F32),
            pltpu.VMEM((V7X_SUBLANES - 1, tr + 2 * HALO - V7X_SUBLANES, V7X_LANES), F32),
        ],
        compiler_params=_params("parallel"),
        name="conv_branch",
    )(u, u, u, u, u, u, conv_w, conv_b3, ln_g3, ln_b3)


def _rope(x, cos, sin):
    half = V7X_LANES
    parts = []
    for p in range(x.shape[1] // half):
        xp = x[:, p * half:(p + 1) * half]
        parts.append(pltpu.roll(xp, half // 2, 1))
    return x * cos + jnp.concatenate(parts, axis=1) * sin


def _decay_terms(lgf, lgb, c):
    ia = lax.broadcasted_iota(jnp.int32, (c, c), 0)
    ib = lax.broadcasted_iota(jnp.int32, (c, c), 1)
    rel = (ia - ib).astype(F32)
    dm = jnp.where(rel >= 0, jnp.exp(lgf * jnp.maximum(rel, 0.0)), 0.0)
    dm = dm + jnp.where(rel <= 0, jnp.exp(lgb * jnp.maximum(-rel, 0.0)), 0.0)
    pos = lax.broadcasted_iota(jnp.int32, (c, 1), 0).astype(F32)
    return dm, pos


def _ret_kernel(lgf_ref, lgb_ref, q_ref, k_ref, v_ref, g_ref, cos_ref, sin_ref, s0f_ref, s0b_ref,
                y_ref, stf_ref, stb_ref, o_scr, q_scr, k_scr, sf_ref, sb_ref, *, layer, n_heads, rope, zero_init):
    h = pl.program_id(1)
    t, dh = q_ref.shape
    c = ROW_TILE
    n = t // c
    lgf = lgf_ref[layer * n_heads + h]
    lgb = lgb_ref[layer * n_heads + h]
    dm, pos = _decay_terms(lgf, lgb, c)
    xi_f = jnp.exp(lgf * (pos + 1.0))
    zeta_f = jnp.exp(lgf * (c - 1.0 - pos))
    xi_b = jnp.exp(lgb * (c - pos))
    zeta_b = jnp.exp(lgb * pos)
    full = jnp.full((1, 1), float(c), F32)
    cd_f = jnp.exp(lgf * full)
    cd_b = jnp.exp(lgb * full)
    scale = dh ** -0.5
    if zero_init:
        sf_ref[...] = jnp.zeros_like(sf_ref)
        sb_ref[...] = jnp.zeros_like(sb_ref)
    else:
        sf_ref[...] = s0f_ref[...]
        sb_ref[...] = s0b_ref[...]

    def fwd(i, carry):
        r = pl.ds(pl.multiple_of(i * c, c), c)
        q = q_ref[r, :]
        k = k_ref[r, :]
        if rope:
            q = _rope(q, cos_ref[r, :], sin_ref[r, :])
            k = _rope(k, cos_ref[r, :], sin_ref[r, :])
        qb = (q * scale).astype(BF16)
        vb = v_ref[r, :].astype(BF16)
        q_scr[r, :] = qb
        k_scr[r, :] = k
        s = _dot_nt(qb, k.astype(BF16)) * dm
        o = _dot(s.astype(BF16), vb) + xi_f * _dot(qb, sf_ref[...].astype(BF16))
        o_scr[r, :] = o
        sf_ref[...] = sf_ref[...] * cd_f + _dot_tn((k * zeta_f).astype(BF16), vb)
        return carry

    lax.fori_loop(0, n, fwd, 0)

    def bwd(j, carry):
        i = n - 1 - j
        r = pl.ds(pl.multiple_of(i * c, c), c)
        qb = q_scr[r, :]
        vb = v_ref[r, :].astype(BF16)
        o = o_scr[r, :] + xi_b * _dot(qb, sb_ref[...].astype(BF16))
        sb_ref[...] = sb_ref[...] * cd_b + _dot_tn((k_scr[r, :] * zeta_b).astype(BF16), vb)
        o = o * lax.rsqrt(jnp.mean(o * o, axis=-1, keepdims=True) + NORM_EPS)
        y_ref[r, :] = (o * _silu(g_ref[r, :])).astype(y_ref.dtype)
        return carry

    lax.fori_loop(0, n, bwd, 0)
    stf_ref[...] = sf_ref[...]
    stb_ref[...] = sb_ref[...]


def retention(lgf, lgb, u, cos, sin, s0f, s0b, layer, n_batch, t, row_off, q_off, n_heads, dh, rope, zero_init):
    rw = n_heads * dh
    ro = row_off // t
    qo, ko, vo, go = ((q_off + m * rw) // dh for m in range(4))
    kern = functools.partial(_ret_kernel, layer=layer, n_heads=n_heads, rope=rope, zero_init=zero_init)
    col = lambda off: pl.BlockSpec((t, dh), lambda b, h, *_: (b + ro, off + h))
    tab = pl.BlockSpec((t, dh), lambda b, h, *_: (0, 0))
    st = pl.BlockSpec((None, None, dh, dh), lambda b, h, *_: (b, h, 0, 0))
    grid_spec = pltpu.PrefetchScalarGridSpec(
        num_scalar_prefetch=2,
        grid=(n_batch, n_heads),
        in_specs=[col(qo), col(ko), col(vo), col(go), tab, tab, st, st],
        out_specs=[pl.BlockSpec((t, dh), lambda b, h, *_: (b, h)), st, st],
        scratch_shapes=[
            pltpu.VMEM((t, dh), F32), pltpu.VMEM((t, dh), BF16), pltpu.VMEM((t, dh), F32),
            pltpu.VMEM((dh, dh), F32), pltpu.VMEM((dh, dh), F32),
        ],
    )
    st_shape = jax.ShapeDtypeStruct((n_batch, n_heads, dh, dh), F32)
    return pl.pallas_call(
        kern,
        grid_spec=grid_spec,
        out_shape=[jax.ShapeDtypeStruct((n_batch * t, rw), BF16), st_shape, st_shape],
        compiler_params=_params("parallel", "parallel"),
        name="retention",
    )(lgf, lgb, u, u, u, u, cos, sin, s0f, s0b)


def _ctx_state_kernel(lgf_ref, lgb_ref, k_ref, v_ref, stf_ref, stb_ref, *, layer, n_heads):
    h = pl.program_id(1)
    t = k_ref.shape[0]
    lgf = lgf_ref[layer * n_heads + h]
    lgb = lgb_ref[layer * n_heads + h]
    pos = lax.broadcasted_iota(jnp.int32, (t, 1), 0).astype(F32)
    k = k_ref[...]
    vb = v_ref[...].astype(BF16)
    stf_ref[...] = _dot_tn((k * jnp.exp(lgf * (t - 1.0 - pos))).astype(BF16), vb)
    stb_ref[...] = _dot_tn((k * jnp.exp(lgb * pos)).astype(BF16), vb)


def ctx_states(lgf, lgb, kv, layer, n_batch, t, n_heads, dh):
    kern = functools.partial(_ctx_state_kernel, layer=layer, n_heads=n_heads)
    st = pl.BlockSpec((None, None, dh, dh), lambda b, h, *_: (b, h, 0, 0))
    grid_spec = pltpu.PrefetchScalarGridSpec(
        num_scalar_prefetch=2,
        grid=(n_batch, n_heads),
        in_specs=[
            pl.BlockSpec((t, dh), lambda b, h, *_: (b, h)),
            pl.BlockSpec((t, dh), lambda b, h, *_: (b, n_heads + h)),
        ],
        out_specs=[st, st],
    )
    st_shape = jax.ShapeDtypeStruct((n_batch, n_heads, dh, dh), F32)
    return pl.pallas_call(
        kern,
        grid_spec=grid_spec,
        out_shape=[st_shape, st_shape],
        compiler_params=_params("parallel", "parallel"),
        name="ctx_states",
    )(lgf, lgb, kv, kv)


def _row_copy(src_ref, src_row, dst_ref, dst_row, sem):
    return pltpu.make_async_copy(src_ref.at[pl.ds(src_row, 1), :], dst_ref.at[pl.ds(dst_row, 1), :], sem)


def _rows_done(like_ref, n_rows, sem):
    span = like_ref.at[pl.ds(0, n_rows), :]
    return pltpu.make_async_copy(span, span, sem)


def _dispatch_kernel(slot_ref, ztile_ref, h_ref, xs_ref, zero_ref, sem_ref, zsem_ref, *, n_rows, n_exp, bmx):
    i = pl.program_id(0)
    tt = h_ref.shape[0]

    n_tiles = xs_ref.shape[0] // bmx

    def zero_copy(tile):
        start = pl.multiple_of(tile * bmx, bmx)
        return pltpu.make_async_copy(zero_ref, xs_ref.at[pl.ds(start, bmx), :], zsem_ref)

    @pl.when(i == 0)
    def _():
        zero_ref[...] = jnp.zeros_like(zero_ref)
        n_used = ztile_ref[n_exp]

        def start(e, carry):
            @pl.when(ztile_ref[e] >= 0)
            def _():
                zero_copy(ztile_ref[e]).start()
            return carry

        def wait(e, carry):
            @pl.when(ztile_ref[e] >= 0)
            def _():
                zero_copy(ztile_ref[e]).wait()
            return carry

        def start_tail(j, carry):
            zero_copy(j).start()
            return carry

        def wait_tail(j, carry):
            zero_copy(j).wait()
            return carry

        lax.fori_loop(0, n_exp, start, 0)
        lax.fori_loop(n_used, n_tiles, start_tail, 0)
        lax.fori_loop(0, n_exp, wait, 0)
        lax.fori_loop(n_used, n_tiles, wait_tail, 0)

    def issue(t, carry):
        for k in range(TOP_K):
            _row_copy(h_ref, t, xs_ref, slot_ref[k * n_rows + i * tt + t], sem_ref).start()
        return carry

    lax.fori_loop(0, tt, issue, 0)
    _rows_done(xs_ref, TOP_K * tt, sem_ref).wait()


def moe_dispatch(slots, ztile, h, n_slots, bmx):
    rows, d = h.shape
    n_exp = ztile.shape[0] - 1
    tt = ROW_TILE
    kern = functools.partial(_dispatch_kernel, n_rows=rows, n_exp=n_exp, bmx=bmx)
    grid_spec = pltpu.PrefetchScalarGridSpec(
        num_scalar_prefetch=2,
        grid=(rows // tt,),
        in_specs=[pl.BlockSpec((tt, d), lambda i, *_: (i, 0))],
        out_specs=pl.BlockSpec(memory_space=pl.ANY),
        scratch_shapes=[pltpu.VMEM((bmx, d), h.dtype), pltpu.SemaphoreType.DMA(()), pltpu.SemaphoreType.DMA(())],
    )
    return pl.pallas_call(
        kern,
        grid_spec=grid_spec,
        out_shape=jax.ShapeDtypeStruct((n_slots, d), h.dtype),
        compiler_params=_params("arbitrary"),
        name="moe_dispatch",
    )(slots, ztile, h)


def _experts_kernel(texp_ref, nused_ref, x_ref, w1_ref, w3_ref, w2_ref, o_ref, w1b_ref, w3b_ref, w2b_ref):
    i = pl.program_id(0)
    valid = i < nused_ref[0]
    fresh = (i == 0) | (valid & (texp_ref[i] != texp_ref[jnp.maximum(i - 1, 0)]))

    @pl.when(fresh)
    def _():
        w1b_ref[...] = w1_ref[...].astype(BF16)
        w3b_ref[...] = w3_ref[...].astype(BF16)
        w2b_ref[...] = w2_ref[...].astype(BF16)

    @pl.when(valid)
    def _():
        x = _unpack_bf16(x_ref[...])
        a = _silu(_dot(x, w1b_ref[...])) * _dot(x, w3b_ref[...])
        o_ref[...] = _pack_halves(_dot(a.astype(BF16), w2b_ref[...]))

    @pl.when(jnp.logical_not(valid))
    def _():
        o_ref[...] = jnp.zeros_like(o_ref)


def moe_experts(tile_expert, n_used, xs, w1, w3, w2, layer, bmx):
    n_slots, dp = xs.shape
    d, hid = w1.shape[2], w1.shape[3]
    tile = lambda i, texp, nused: jnp.minimum(i, nused[0] - 1)
    rows = pl.BlockSpec((bmx, dp), lambda i, texp, nused: (tile(i, texp, nused), 0))
    out_rows = pl.BlockSpec((bmx, dp), lambda i, texp, nused: (i, 0))
    up = pl.BlockSpec((None, None, d, hid), lambda i, texp, nused: (layer, texp[tile(i, texp, nused)], 0, 0))
    down = pl.BlockSpec((None, None, hid, d), lambda i, texp, nused: (layer, texp[tile(i, texp, nused)], 0, 0))
    grid_spec = pltpu.PrefetchScalarGridSpec(
        num_scalar_prefetch=2,
        grid=(n_slots // bmx,),
        in_specs=[rows, up, up, down],
        out_specs=out_rows,
        scratch_shapes=[pltpu.VMEM((d, hid), BF16), pltpu.VMEM((d, hid), BF16), pltpu.VMEM((hid, d), BF16)],
    )
    return pl.pallas_call(
        _experts_kernel,
        grid_spec=grid_spec,
        out_shape=jax.ShapeDtypeStruct((n_slots, dp), xs.dtype),
        compiler_params=_params("arbitrary"),
        name="moe_experts",
    )(tile_expert, n_used, xs, w1, w3, w2)


def _combine_kernel(slot_ref, w_ref, ys_ref, o_ref, buf_ref, sem_ref, *, n_rows):
    i = pl.program_id(0)
    n = pl.num_programs(0)
    tt = o_ref.shape[0]

    def issue(tile, buf):
        def body(t, carry):
            for k in range(TOP_K):
                _row_copy(ys_ref, slot_ref[k * n_rows + tile * tt + t], buf_ref.at[buf], k * tt + t,
                          sem_ref.at[buf]).start()
            return carry

        lax.fori_loop(0, tt, body, 0)

    @pl.when(i == 0)
    def _():
        issue(0, 0)

    @pl.when(i + 1 < n)
    def _():
        issue(i + 1, (i + 1) % 2)

    buf = i % 2
    _rows_done(ys_ref, TOP_K * tt, sem_ref.at[buf]).wait()
    w = w_ref[...]
    half = ys_ref.shape[1]
    acc_lo = jnp.zeros((tt, half), F32)
    acc_hi = jnp.zeros((tt, half), F32)
    for k in range(TOP_K):
        lo, hi = _unpack_halves(buf_ref[buf, k * tt:(k + 1) * tt, :])
        acc_lo = acc_lo + lo * w[:, k:k + 1]
        acc_hi = acc_hi + hi * w[:, k:k + 1]
    o_ref[:, :half] = acc_lo
    o_ref[:, half:] = acc_hi


def moe_combine(slots, wts, ys, rows):
    dp = ys.shape[1]
    tt = COMBINE_TT
    kern = functools.partial(_combine_kernel, n_rows=rows)
    grid_spec = pltpu.PrefetchScalarGridSpec(
        num_scalar_prefetch=1,
        grid=(rows // tt,),
        in_specs=[pl.BlockSpec((tt, TOP_K), lambda i, *_: (i, 0)), pl.BlockSpec(memory_space=pl.ANY)],
        out_specs=pl.BlockSpec((tt, 2 * dp), lambda i, *_: (i, 0)),
        scratch_shapes=[pltpu.VMEM((2, TOP_K * tt, dp), ys.dtype), pltpu.SemaphoreType.DMA((2,))],
    )
    return pl.pallas_call(
        kern,
        grid_spec=grid_spec,
        out_shape=jax.ShapeDtypeStruct((rows, 2 * dp), F32),
        compiler_params=_params("arbitrary"),
        name="moe_combine",
    )(slots, wts, ys)


def routing_tables(eidx, pos, counts, bmx, n_tiles):
    n_exp = counts.shape[0]
    padded = (counts + bmx - 1) // bmx * bmx
    ends = jnp.cumsum(padded)
    starts = ends - padded
    experts = jnp.arange(n_exp, dtype=jnp.int32)[:, None, None]
    slots = (jnp.sum(jnp.where(eidx[None] == experts, starts[:, None, None], 0), axis=0) + pos).reshape(-1)
    tile_ends = ends // bmx
    n_used = tile_ends[-1:]
    tiles = jnp.arange(n_tiles, dtype=jnp.int32)
    tile_expert = jnp.sum((tile_ends[None, :] <= tiles[:, None]).astype(jnp.int32), axis=1)
    tile_expert = jnp.minimum(tile_expert, n_exp - 1).astype(jnp.int32)
    ztile = jnp.concatenate([jnp.where(counts > 0, tile_ends - 1, -1), n_used]).astype(jnp.int32)
    return slots.astype(jnp.int32), ztile, tile_expert, n_used.astype(jnp.int32)


def _glu_mm_kernel(h_ref, w1_ref, w3_ref, o_ref, w1b_ref, w3b_ref):
    @pl.when(pl.program_id(1) == 0)
    def _():
        w1b_ref[...] = w1_ref[...].astype(BF16)
        w3b_ref[...] = w3_ref[...].astype(BF16)

    h = _unpack_bf16(h_ref[...])
    o_ref[...] = (_silu(_dot(h, w1b_ref[...])) * _dot(h, w3b_ref[...])).astype(o_ref.dtype)


def glu_matmul(h, w1, w3, layer, rows, bm):
    k = w1.shape[1]
    n = w1.shape[2]
    bn = HID_CHUNK
    return pl.pallas_call(
        _glu_mm_kernel,
        grid=(n // bn, rows // bm),
        in_specs=[
            pl.BlockSpec((bm, h.shape[1]), lambda j, i: (i, 0)),
            pl.BlockSpec((None, k, bn), lambda j, i: (layer, 0, j)),
            pl.BlockSpec((None, k, bn), lambda j, i: (layer, 0, j)),
        ],
        out_specs=pl.BlockSpec((bm, bn), lambda j, i: (i, j)),
        out_shape=jax.ShapeDtypeStruct((rows, n), BF16),
        scratch_shapes=[pltpu.VMEM((k, bn), BF16), pltpu.VMEM((k, bn), BF16)],
        compiler_params=_params("arbitrary", "arbitrary"),
        name="glu_matmul",
    )(h, w1, w3)


def _rope_tables(seq, dh):
    rows = seq // GRID_W
    row = jnp.repeat(jnp.arange(rows, dtype=F32), GRID_W)
    col = jnp.tile(jnp.arange(GRID_W, dtype=F32), rows)
    quarter = dh // 4
    inv_freq = ROPE_BASE ** (-jnp.arange(quarter, dtype=F32) / quarter)
    ang_r = row[:, None] * inv_freq[None, :]
    ang_c = col[:, None] * inv_freq[None, :]
    cos = jnp.concatenate([jnp.cos(ang_r), jnp.cos(ang_r), jnp.cos(ang_c), jnp.cos(ang_c)], axis=-1)
    sin = jnp.concatenate([-jnp.sin(ang_r), jnp.sin(ang_r), -jnp.sin(ang_c), jnp.sin(ang_c)], axis=-1)
    return cos, sin


def kernel(x, c, ctx, c_ctx, mod_w, mod_b, norm1_g, norm2_g, w_in, conv_w, conv_b, conv_ln_g, conv_ln_b,
           w_conv_out, ret_log_gamma_fwd, ret_log_gamma_bwd, w_ret_out, w_merge_out, router_w, router_bias,
           exp_w1, exp_w3, exp_w2, shared_w1, shared_w3, shared_w2, final_g):
    b, s, d = x.shape
    n_ctx = ctx.shape[1]
    depth = mod_w.shape[0]
    cc = conv_w.shape[2]
    n_heads = ret_log_gamma_fwd.shape[1]
    rw = w_ret_out.shape[1]
    dh = rw // n_heads
    n_exp = router_w.shape[2]
    n_lat, n_cx = b * s, b * n_ctx
    n_all = n_lat + n_cx
    bm = min(MM_BM, s, n_cx)
    assert b + 1 <= 8 and s % ROW_TILE == 0 and n_ctx % ROW_TILE == 0 and dh == 2 * V7X_LANES
    assert s % bm == 0 and n_cx % bm == 0 and s % MOE_TM == 0 and n_cx % MOE_TM == 0
    q_off, gate_off = 2 * cc, 2 * cc + 4 * rw

    xa = jnp.concatenate([x.reshape(n_lat, d), ctx.reshape(n_cx, d)], axis=0)
    c8 = jnp.zeros((8, d), F32).at[:b].set(c).at[b].set(c_ctx)
    mod_b3 = mod_b.reshape(depth, 1, -1)
    n1g, n2g = norm1_g.reshape(depth, 1, d), norm2_g.reshape(depth, 1, d)
    conv_b3, ln_g3, ln_b3 = (a.reshape(depth, 1, cc) for a in (conv_b, conv_ln_g, conv_ln_b))
    lgf, lgb = ret_log_gamma_fwd.reshape(-1), ret_log_gamma_bwd.reshape(-1)
    rw_t = jnp.swapaxes(router_w, 1, 2)
    rw_hi = rw_t.astype(BF16)
    rw_lo = (rw_t - rw_hi.astype(F32)).astype(BF16)
    rw_cat = jnp.concatenate([rw_hi, rw_lo], axis=1)
    bias3 = router_bias.reshape(depth, n_exp, 1)
    cos, sin = _rope_tables(s, dh)
    zero_st = jnp.zeros((b, n_heads, dh, dh), F32)

    for layer in range(depth):
        last = layer == depth - 1
        rows = n_lat if last else n_all
        seq_args = dict(n_lat_rows=n_lat, seq=s, n_batch=b)
        mod3 = modulation(c8, mod_w, mod_b3, layer).reshape(8 * N_MOD, 1, d)
        h1 = norm_mod(xa, n1g, mod3, layer, 0, 1, n_all, **seq_args)
        if last:
            u = matmul(h1, w_in, layer, n_lat, 0, 0, w_in.shape[2], bm)
            kv_c = matmul(h1, w_in, layer, n_cx, n_lat, q_off + rw, 2 * rw, bm)
            st_f, st_b = ctx_states(lgf, lgb, kv_c, layer, b, n_ctx, n_heads, dh)
        else:
            u = matmul(h1, w_in, layer, n_all, 0, 0, w_in.shape[2], bm)
            yr_c, st_f, st_b = retention(lgf, lgb, u, cos, sin, zero_st, zero_st, layer, b, n_ctx, n_lat, q_off,
                                         n_heads, dh, rope=False, zero_init=True)
        yc = conv_branch(u, conv_w, conv_b3, ln_g3, ln_b3, layer, rows, n_lat, s, n_ctx)
        yr, _, _ = retention(lgf, lgb, u, cos, sin, st_f, st_b, layer, b, s, 0, q_off, n_heads, dh,
                             rope=True, zero_init=False)
        if not last:
            yr = jnp.concatenate([yr, yr_c], axis=0)
        mixed = merge_branches(yc, yr, u, w_conv_out, w_ret_out, layer, rows, gate_off, bm)
        xa = resid_matmul(mixed, w_merge_out, xa, mod3, layer, 2, rows, bm=bm, **seq_args)
        h2, eidx, pos, wts, cnt = norm_router(xa, n2g, mod3, rw_cat, rw_hi, bias3, layer, 3, 4, rows, **seq_args)
        n_tiles = TOP_K * rows // MOE_BMX + n_exp
        slots, ztile, tile_expert, n_used = routing_tables(eidx, pos, cnt[:, 0].astype(jnp.int32), MOE_BMX, n_tiles)
        xs = moe_dispatch(slots, ztile, h2, n_tiles * MOE_BMX, MOE_BMX)
        ys = moe_experts(tile_expert, n_used, xs, exp_w1, exp_w3, exp_w2, layer, MOE_BMX)
        routed = moe_combine(slots, wts.T, ys, rows)
        act = glu_matmul(h2, shared_w1, shared_w3, layer, rows, MOE_TM)
        xa = resid_matmul(act, shared_w2, xa, mod3, layer, 5, rows, bm=bm, extra=routed, **seq_args)
    out = final_norm(xa, final_g.reshape(1, d), n_lat)
    return out.reshape(b, s, d)
```

```python
import functools

import jax
import jax.numpy as jnp
from jax import lax
from jax.experimental import pallas as pl
from jax.experimental.pallas import tpu as pltpu

GRID_W = 64
N_GROUPS = 8
TOPK_GROUPS = 4
TOP_K = 8
ROUTED_SCALE = 2.5
N_MOD = 6
NORM_EPS = 1e-6
ROPE_BASE = 10000.0

V7X_LANES = 128
V7X_SUBLANES = 8
V7X_VMEM_BYTES = 64 * 1024 * 1024
VMEM_LIMIT = V7X_VMEM_BYTES - 8 * 1024 * 1024

ROW_TILE = 256
HALO = 16
MM_BM = 1024
MM_BN = 512
MOE_TM = 512
MOE_BMX = 256
COMBINE_TT = 128
HID_CHUNK = 256

BF16 = jnp.bfloat16
F32 = jnp.float32


def _params(*sem):
    return pltpu.CompilerParams(dimension_semantics=sem, vmem_limit_bytes=VMEM_LIMIT)


def _dot(a, b):
    return jnp.dot(a, b, preferred_element_type=F32)


def _dot_nt(a, b):
    return lax.dot_general(a, b, (((1,), (1,)), ((), ())), preferred_element_type=F32)


def _dot_tn(a, b):
    return lax.dot_general(a, b, (((0,), (0,)), ((), ())), preferred_element_type=F32)


def _silu(x):
    return x * jax.nn.sigmoid(x)


def _pack_halves(v):
    half = v.shape[1] // 2
    lo = pltpu.bitcast(v[:, :half].astype(BF16).astype(F32), jnp.uint32)
    hi = pltpu.bitcast(v[:, half:].astype(BF16).astype(F32), jnp.uint32)
    return (hi & jnp.uint32(0xFFFF0000)) | (lo >> 16)


def _unpack_halves(p):
    lo = pltpu.bitcast(p << 16, F32)
    hi = pltpu.bitcast(p & jnp.uint32(0xFFFF0000), F32)
    return lo, hi


def _unpack_bf16(p):
    lo, hi = _unpack_halves(p)
    return jnp.concatenate([lo.astype(BF16), hi.astype(BF16)], axis=1)


def _mod_row(i, n_lat_tiles, tiles_per_seq, n_batch):
    return jnp.where(i < n_lat_tiles, i // tiles_per_seq, n_batch)


def _mod_kernel(c_ref, w_ref, b_ref, o_ref):
    c = c_ref[...]
    sc = _silu(c).astype(BF16)
    o_ref[...] = _dot(sc, w_ref[...].astype(BF16)) + b_ref[...]


def modulation(c8, mod_w, mod_b3, layer):
    d = c8.shape[1]
    n = mod_w.shape[2]
    bn = MM_BN
    return pl.pallas_call(
        _mod_kernel,
        grid=(n // bn,),
        in_specs=[
            pl.BlockSpec((8, d), lambda j: (0, 0)),
            pl.BlockSpec((None, d, bn), lambda j: (layer, 0, j)),
            pl.BlockSpec((None, 1, bn), lambda j: (layer, 0, j)),
        ],
        out_specs=pl.BlockSpec((8, bn), lambda j: (0, j)),
        out_shape=jax.ShapeDtypeStruct((8, n), F32),
        compiler_params=_params("arbitrary"),
        name="modulation",
    )(c8, mod_w, mod_b3)


def _rms(x, g):
    return x * lax.rsqrt(jnp.mean(x * x, axis=-1, keepdims=True) + NORM_EPS) * g


def _norm_mod_kernel(x_ref, g_ref, sh_ref, sc_ref, o_ref):
    h = _rms(x_ref[...], g_ref[...]) * (1.0 + sc_ref[...]) + sh_ref[...]
    o_ref[...] = h.astype(o_ref.dtype)


def norm_mod(x, g3, mod3, layer, sh_idx, sc_idx, rows, n_lat_rows, seq, n_batch):
    d = x.shape[1]
    tr = ROW_TILE
    row = functools.partial(_mod_row, n_lat_tiles=n_lat_rows // tr, tiles_per_seq=seq // tr, n_batch=n_batch)
    return pl.pallas_call(
        _norm_mod_kernel,
        grid=(rows // tr,),
        in_specs=[
            pl.BlockSpec((tr, d), lambda i: (i, 0)),
            pl.BlockSpec((None, 1, d), lambda i: (layer, 0, 0)),
            pl.BlockSpec((None, 1, d), lambda i: (row(i) * N_MOD + sh_idx, 0, 0)),
            pl.BlockSpec((None, 1, d), lambda i: (row(i) * N_MOD + sc_idx, 0, 0)),
        ],
        out_specs=pl.BlockSpec((tr, d), lambda i: (i, 0)),
        out_shape=jax.ShapeDtypeStruct((rows, d), BF16),
        compiler_params=_params("parallel"),
        name="norm_mod",
    )(x, g3, mod3, mod3)


def _final_norm_kernel(x_ref, g_ref, o_ref):
    o_ref[...] = _rms(x_ref[...], g_ref[...])


def final_norm(x, g2, rows):
    d = x.shape[1]
    tr = ROW_TILE
    return pl.pallas_call(
        _final_norm_kernel,
        grid=(rows // tr,),
        in_specs=[pl.BlockSpec((tr, d), lambda i: (i, 0)), pl.BlockSpec((1, d), lambda i: (0, 0))],
        out_specs=pl.BlockSpec((tr, d), lambda i: (i, 0)),
        out_shape=jax.ShapeDtypeStruct((rows, d), F32),
        compiler_params=_params("parallel"),
        name="final_norm",
    )(x, g2)


def _route(logits_t, bias):
    n_exp, tt = logits_t.shape
    per = n_exp // N_GROUPS
    s = jax.nn.sigmoid(logits_t)
    biased = s + bias
    g3 = biased.reshape(N_GROUPS, per, tt)
    mem = lax.broadcasted_iota(jnp.int32, g3.shape, 1)
    m1 = jnp.max(g3, axis=1, keepdims=True)
    first = jnp.min(jnp.where(g3 == m1, mem, per), axis=1, keepdims=True)
    m2 = jnp.max(jnp.where(mem == first, -jnp.inf, g3), axis=1, keepdims=True)
    gs = m1 + m2
    gid = lax.broadcasted_iota(jnp.int32, gs.shape, 0)
    grank = jnp.zeros(gs.shape, jnp.int32)
    for j in range(N_GROUPS):
        other = gs[j:j + 1]
        ahead = (other > gs) | ((other == gs) & (gid > j))
        grank = grank + ahead.astype(jnp.int32)
    gmask = grank < TOPK_GROUPS
    masked = jnp.where(gmask, g3, -jnp.inf).reshape(n_exp, tt)
    eid = lax.broadcasted_iota(jnp.int32, masked.shape, 0)
    rank = jnp.zeros(masked.shape, jnp.int32)
    for j in range(n_exp):
        other = masked[j:j + 1, :]
        ahead = (other > masked) | ((other == masked) & (eid > j))
        rank = rank + ahead.astype(jnp.int32)
    sel = jnp.where(rank < TOP_K, s, 0.0)
    return sel / jnp.sum(sel, axis=0, keepdims=True) * ROUTED_SCALE, rank


def _norm_router_kernel(x_ref, g_ref, sh_ref, sc_ref, wcat_ref, whi_ref, bias_ref,
                        h_ref, eidx_ref, pos_ref, wts_ref, cnt_ref):
    i = pl.program_id(0)
    h = _rms(x_ref[...], g_ref[...]) * (1.0 + sc_ref[...]) + sh_ref[...]
    h_hi = h.astype(BF16)
    h_lo = (h - h_hi.astype(F32)).astype(BF16)
    h_ref[...] = _pack_halves(h)
    n_exp = whi_ref.shape[0]
    lt = _dot_nt(wcat_ref[...], h_hi)
    logits_t = lt[:n_exp] + lt[n_exp:] + _dot_nt(whi_ref[...], h_lo)
    comb, rank = _route(logits_t, bias_ref[...])
    tt = comb.shape[1]

    @pl.when(i == 0)
    def _():
        cnt_ref[...] = jnp.zeros_like(cnt_ref)

    chosen = (rank < TOP_K).astype(F32)
    ta = lax.broadcasted_iota(jnp.int32, (tt, tt), 0)
    tb = lax.broadcasted_iota(jnp.int32, (tt, tt), 1)
    before = (ta < tb).astype(BF16)
    prefix = _dot(chosen.astype(BF16), before)
    base = cnt_ref[:, 0:1]
    pos = base + prefix
    cnt_ref[...] = cnt_ref[...] + jnp.sum(chosen, axis=1, keepdims=True)
    eid = lax.broadcasted_iota(jnp.int32, comb.shape, 0).astype(F32)
    for k in range(TOP_K):
        mk = rank == k
        eidx_ref[k:k + 1, :] = jnp.sum(jnp.where(mk, eid, 0.0), axis=0, keepdims=True).astype(jnp.int32)
        pos_ref[k:k + 1, :] = jnp.sum(jnp.where(mk, pos, 0.0), axis=0, keepdims=True).astype(jnp.int32)
        wts_ref[k:k + 1, :] = jnp.sum(jnp.where(mk, comb, 0.0), axis=0, keepdims=True)


def norm_router(x, g3, mod3, wcat_t, whi_t, bias3, layer, sh_idx, sc_idx, rows, n_lat_rows, seq, n_batch):
    d = x.shape[1]
    n_exp = whi_t.shape[1]
    tr = ROW_TILE
    row = functools.partial(_mod_row, n_lat_tiles=n_lat_rows // tr, tiles_per_seq=seq // tr, n_batch=n_batch)
    lists = pl.BlockSpec((TOP_K, tr), lambda i: (0, i))
    return pl.pallas_call(
        _norm_router_kernel,
        grid=(rows // tr,),
        in_specs=[
            pl.BlockSpec((tr, d), lambda i: (i, 0)),
            pl.BlockSpec((None, 1, d), lambda i: (layer, 0, 0)),
            pl.BlockSpec((None, 1, d), lambda i: (row(i) * N_MOD + sh_idx, 0, 0)),
            pl.BlockSpec((None, 1, d), lambda i: (row(i) * N_MOD + sc_idx, 0, 0)),
            pl.BlockSpec((None, 2 * n_exp, d), lambda i: (layer, 0, 0)),
            pl.BlockSpec((None, n_exp, d), lambda i: (layer, 0, 0)),
            pl.BlockSpec((None, n_exp, 1), lambda i: (layer, 0, 0)),
        ],
        out_specs=[
            pl.BlockSpec((tr, d // 2), lambda i: (i, 0)),
            lists, lists, lists,
            pl.BlockSpec((n_exp, V7X_LANES), lambda i: (0, 0)),
        ],
        out_shape=[
            jax.ShapeDtypeStruct((rows, d // 2), jnp.uint32),
            jax.ShapeDtypeStruct((TOP_K, rows), jnp.int32),
            jax.ShapeDtypeStruct((TOP_K, rows), jnp.int32),
            jax.ShapeDtypeStruct((TOP_K, rows), F32),
            jax.ShapeDtypeStruct((n_exp, V7X_LANES), F32),
        ],
        compiler_params=_params("arbitrary"),
        name="norm_router",
    )(x, g3, mod3, mod3, wcat_t, whi_t, bias3)


def _mm_kernel(a_ref, w_ref, o_ref, wb_ref):
    @pl.when(pl.program_id(1) == 0)
    def _():
        wb_ref[...] = w_ref[...].astype(BF16)

    o_ref[...] = _dot(a_ref[...], wb_ref[...]).astype(o_ref.dtype)


def matmul(a, w, layer, rows, row_off, col_off, ncols, bm, out_dtype=F32):
    k = a.shape[1]
    bn = MM_BN
    ro, co = row_off // bm, col_off // bn
    return pl.pallas_call(
        _mm_kernel,
        grid=(ncols // bn, rows // bm),
        in_specs=[
            pl.BlockSpec((bm, k), lambda j, i: (i + ro, 0)),
            pl.BlockSpec((None, k, bn), lambda j, i: (layer, 0, j + co)),
        ],
        out_specs=pl.BlockSpec((bm, bn), lambda j, i: (i, j)),
        out_shape=jax.ShapeDtypeStruct((rows, ncols), out_dtype),
        scratch_shapes=[pltpu.VMEM((k, bn), BF16)],
        compiler_params=_params("arbitrary", "arbitrary"),
        name="matmul",
    )(a, w)


def _merge_kernel(yc_ref, yr_ref, gc_ref, gr_ref, wc_ref, wr_ref, o_ref, wcb_ref, wrb_ref):
    @pl.when(pl.program_id(1) == 0)
    def _():
        wcb_ref[...] = wc_ref[...].astype(BF16)
        wrb_ref[...] = wr_ref[...].astype(BF16)

    y = jax.nn.sigmoid(gc_ref[...]) * _dot(yc_ref[...], wcb_ref[...])
    y = y + jax.nn.sigmoid(gr_ref[...]) * _dot(yr_ref[...], wrb_ref[...])
    o_ref[...] = y.astype(o_ref.dtype)


def merge_branches(yc, yr, u, w_conv_out, w_ret_out, layer, rows, gate_off, bm):
    kc, kr = yc.shape[1], yr.shape[1]
    d = w_conv_out.shape[2]
    bn = MM_BN
    gco, gro = gate_off // bn, (gate_off + d) // bn
    return pl.pallas_call(
        _merge_kernel,
        grid=(d // bn, rows // bm),
        in_specs=[
            pl.BlockSpec((bm, kc), lambda j, i: (i, 0)),
            pl.BlockSpec((bm, kr), lambda j, i: (i, 0)),
            pl.BlockSpec((bm, bn), lambda j, i: (i, j + gco)),
            pl.BlockSpec((bm, bn), lambda j, i: (i, j + gro)),
            pl.BlockSpec((None, kc, bn), lambda j, i: (layer, 0, j)),
            pl.BlockSpec((None, kr, bn), lambda j, i: (layer, 0, j)),
        ],
        out_specs=pl.BlockSpec((bm, bn), lambda j, i: (i, j)),
        out_shape=jax.ShapeDtypeStruct((rows, d), BF16),
        scratch_shapes=[pltpu.VMEM((kc, bn), BF16), pltpu.VMEM((kr, bn), BF16)],
        compiler_params=_params("arbitrary", "arbitrary"),
        name="merge_branches",
    )(yc, yr, u, u, w_conv_out, w_ret_out)


def _resid_mm_kernel(a_ref, w_ref, x_ref, g_ref, o_ref, wb_ref):
    @pl.when(pl.program_id(1) == 0)
    def _():
        wb_ref[...] = w_ref[...].astype(BF16)

    o_ref[...] = x_ref[...] + g_ref[...] * _dot(a_ref[...], wb_ref[...])


def _resid_mm_add_kernel(a_ref, w_ref, x_ref, g_ref, e_ref, o_ref, wb_ref):
    @pl.when(pl.program_id(1) == 0)
    def _():
        wb_ref[...] = w_ref[...].astype(BF16)

    o_ref[...] = x_ref[...] + g_ref[...] * (e_ref[...] + _dot(a_ref[...], wb_ref[...]))


def resid_matmul(a, w, x, mod3, layer, g_idx, rows, n_lat_rows, seq, n_batch, bm, extra=None):
    k = a.shape[1]
    d = w.shape[2]
    bn = MM_BN
    row = functools.partial(_mod_row, n_lat_tiles=n_lat_rows // bm, tiles_per_seq=seq // bm, n_batch=n_batch)
    tile = pl.BlockSpec((bm, bn), lambda j, i: (i, j))
    in_specs = [
        pl.BlockSpec((bm, k), lambda j, i: (i, 0)),
        pl.BlockSpec((None, k, bn), lambda j, i: (layer, 0, j)),
        tile,
        pl.BlockSpec((None, 1, bn), lambda j, i: (row(i) * N_MOD + g_idx, 0, j)),
    ]
    operands = [a, w, x, mod3]
    if extra is not None:
        in_specs.append(tile)
        operands.append(extra)
    return pl.pallas_call(
        _resid_mm_kernel if extra is None else _resid_mm_add_kernel,
        grid=(d // bn, rows // bm),
        in_specs=in_specs,
        out_specs=tile,
        out_shape=jax.ShapeDtypeStruct((rows, d), F32),
        scratch_shapes=[pltpu.VMEM((k, bn), BF16)],
        compiler_params=_params("arbitrary", "arbitrary"),
        name="resid_matmul",
    )(*operands)


def _conv_kernel(a_ref, g_ref, ap_ref, gp_ref, an_ref, gn_ref, w_ref, b_ref, lng_ref, lnb_ref, o_ref,
                 buf_ref, acc_ref, shift_ref, *, n_lat_tiles, lat_tps, ctx_tps, conv_k):
    i = pl.program_id(0)
    tr, cc = a_ref.shape
    is_lat = i < n_lat_tiles
    pos = jnp.where(is_lat, i % lat_tps, (i - n_lat_tiles) % ctx_tps)
    tps = jnp.where(is_lat, lat_tps, ctx_tps)
    keep_prev = (pos != 0).astype(F32)
    keep_next = (pos != tps - 1).astype(F32)
    buf_ref[HALO:HALO + tr, :] = a_ref[...] * jax.nn.sigmoid(g_ref[...])
    buf_ref[0:HALO, :] = ap_ref[...] * jax.nn.sigmoid(gp_ref[...]) * keep_prev
    buf_ref[HALO + tr:, :] = an_ref[...] * jax.nn.sigmoid(gn_ref[...]) * keep_next
    pad = (conv_k - 1) // 2
    rc = 128
    n_rc = tr // rc
    span = shift_ref.shape[1]

    def chunk(t, carry):
        lanes = pl.ds(pl.multiple_of(t * V7X_LANES, V7X_LANES), V7X_LANES)
        for s in range(1, V7X_SUBLANES):
            shift_ref[s - 1] = buf_ref[s:s + span, lanes]
        for r in range(n_rc):
            acc = jnp.zeros((rc, V7X_LANES), F32)
            for kk in range(conv_k):
                whole, s = divmod(HALO - pad + kk, V7X_SUBLANES)
                start = r * rc + whole * V7X_SUBLANES
                if s == 0:
                    window = buf_ref[start:start + rc, lanes]
                else:
                    window = shift_ref[s - 1, start:start + rc, :]
                acc = acc + window * w_ref[kk:kk + 1, lanes]
            acc_ref[r * rc:(r + 1) * rc, lanes] = acc
        return carry

    lax.fori_loop(0, cc // V7X_LANES, chunk, 0)
    y = acc_ref[...] + b_ref[...]
    mu = jnp.mean(y, axis=-1, keepdims=True)
    yc = y - mu
    var = jnp.mean(yc * yc, axis=-1, keepdims=True)
    z = yc * lax.rsqrt(var + NORM_EPS) * lng_ref[...] + lnb_ref[...]
    o_ref[...] = _silu(z).astype(o_ref.dtype)


def conv_branch(u, conv_w, conv_b3, ln_g3, ln_b3, layer, rows, n_lat_rows, seq, ctx_len):
    conv_k, cc = conv_w.shape[1], conv_w.shape[2]
    tr = ROW_TILE
    hb = tr // HALO
    n_halo_blocks = u.shape[0] // HALO
    kern = functools.partial(_conv_kernel, n_lat_tiles=n_lat_rows // tr, lat_tps=seq // tr,
                             ctx_tps=max(ctx_len // tr, 1), conv_k=conv_k)
    prev = lambda i: jnp.maximum(i * hb - 1, 0)
    nxt = lambda i: jnp.minimum((i + 1) * hb, n_halo_blocks - 1)
    return pl.pallas_call(
        kern,
        grid=(rows // tr,),
        in_specs=[
            pl.BlockSpec((tr, cc), lambda i: (i, 0)),
            pl.BlockSpec((tr, cc), lambda i: (i, 1)),
            pl.BlockSpec((HALO, cc), lambda i: (prev(i), 0)),
            pl.BlockSpec((HALO, cc), lambda i: (prev(i), 1)),
            pl.BlockSpec((HALO, cc), lambda i: (nxt(i), 0)),
            pl.BlockSpec((HALO, cc), lambda i: (nxt(i), 1)),
            pl.BlockSpec((None, conv_k, cc), lambda i: (layer, 0, 0)),
            pl.BlockSpec((None, 1, cc), lambda i: (layer, 0, 0)),
            pl.BlockSpec((None, 1, cc), lambda i: (layer, 0, 0)),
            pl.BlockSpec((None, 1, cc), lambda i: (layer, 0, 0)),
        ],
        out_specs=pl.BlockSpec((tr, cc), lambda i: (i, 0)),
        out_shape=jax.ShapeDtypeStruct((rows, cc), BF16),
        scratch_shapes=[
            pltpu.VMEM((tr + 2 * HALO, cc), F32),
            pltpu.VMEM((tr, cc), F32),
            pltpu.VMEM((V7X_SUBLANES - 1, tr + 2 * HALO - V7X_SUBLANES, V7X_LANES), F32),
        ],
        compiler_params=_params("parallel"),
        name="conv_branch",
    )(u, u, u, u, u, u, conv_w, conv_b3, ln_g3, ln_b3)


def _rope(x, cos, sin):
    half = V7X_LANES
    parts = []
    for p in range(x.shape[1] // half):
        xp = x[:, p * half:(p + 1) * half]
        parts.append(pltpu.roll(xp, half // 2, 1))
    return x * cos + jnp.concatenate(parts, axis=1) * sin


def _decay_terms(lgf, lgb, c):
    ia = lax.broadcasted_iota(jnp.int32, (c, c), 0)
    ib = lax.broadcasted_iota(jnp.int32, (c, c), 1)
    rel = (ia - ib).astype(F32)
    dm = jnp.where(rel >= 0, jnp.exp(lgf * jnp.maximum(rel, 0.0)), 0.0)
    dm = dm + jnp.where(rel <= 0, jnp.exp(lgb * jnp.maximum(-rel, 0.0)), 0.0)
    pos = lax.broadcasted_iota(jnp.int32, (c, 1), 0).astype(F32)
    return dm, pos


def _ret_kernel(lgf_ref, lgb_ref, q_ref, k_ref, v_ref, g_ref, cos_ref, sin_ref, s0f_ref, s0b_ref,
                y_ref, stf_ref, stb_ref, o_scr, q_scr, k_scr, sf_ref, sb_ref, *, layer, n_heads, rope, zero_init):
    h = pl.program_id(1)
    t, dh = q_ref.shape
    c = ROW_TILE
    n = t // c
    lgf = lgf_ref[layer * n_heads + h]
    lgb = lgb_ref[layer * n_heads + h]
    dm, pos = _decay_terms(lgf, lgb, c)
    xi_f = jnp.exp(lgf * (pos + 1.0))
    zeta_f = jnp.exp(lgf * (c - 1.0 - pos))
    xi_b = jnp.exp(lgb * (c - pos))
    zeta_b = jnp.exp(lgb * pos)
    full = jnp.full((1, 1), float(c), F32)
    cd_f = jnp.exp(lgf * full)
    cd_b = jnp.exp(lgb * full)
    scale = dh ** -0.5
    if zero_init:
        sf_ref[...] = jnp.zeros_like(sf_ref)
        sb_ref[...] = jnp.zeros_like(sb_ref)
    else:
        sf_ref[...] = s0f_ref[...]
        sb_ref[...] = s0b_ref[...]

    def fwd(i, carry):
        r = pl.ds(pl.multiple_of(i * c, c), c)
        q = q_ref[r, :]
        k = k_ref[r, :]
        if rope:
            q = _rope(q, cos_ref[r, :], sin_ref[r, :])
            k = _rope(k, cos_ref[r, :], sin_ref[r, :])
        qb = (q * scale).astype(BF16)
        vb = v_ref[r, :].astype(BF16)
        q_scr[r, :] = qb
        k_scr[r, :] = k
        s = _dot_nt(qb, k.astype(BF16)) * dm
        o = _dot(s.astype(BF16), vb) + xi_f * _dot(qb, sf_ref[...].astype(BF16))
        o_scr[r, :] = o
        sf_ref[...] = sf_ref[...] * cd_f + _dot_tn((k * zeta_f).astype(BF16), vb)
        return carry

    lax.fori_loop(0, n, fwd, 0)

    def bwd(j, carry):
        i = n - 1 - j
        r = pl.ds(pl.multiple_of(i * c, c), c)
        qb = q_scr[r, :]
        vb = v_ref[r, :].astype(BF16)
        o = o_scr[r, :] + xi_b * _dot(qb, sb_ref[...].astype(BF16))
        sb_ref[...] = sb_ref[...] * cd_b + _dot_tn((k_scr[r, :] * zeta_b).astype(BF16), vb)
        o = o * lax.rsqrt(jnp.mean(o * o, axis=-1, keepdims=True) + NORM_EPS)
        y_ref[r, :] = (o * _silu(g_ref[r, :])).astype(y_ref.dtype)
        return carry

    lax.fori_loop(0, n, bwd, 0)
    stf_ref[...] = sf_ref[...]
    stb_ref[...] = sb_ref[...]


def retention(lgf, lgb, u, cos, sin, s0f, s0b, layer, n_batch, t, row_off, q_off, n_heads, dh, rope, zero_init):
    rw = n_heads * dh
    ro = row_off // t
    qo, ko, vo, go = ((q_off + m * rw) // dh for m in range(4))
    kern = functools.partial(_ret_kernel, layer=layer, n_heads=n_heads, rope=rope, zero_init=zero_init)
    col = lambda off: pl.BlockSpec((t, dh), lambda b, h, *_: (b + ro, off + h))
    tab = pl.BlockSpec((t, dh), lambda b, h, *_: (0, 0))
    st = pl.BlockSpec((None, None, dh, dh), lambda b, h, *_: (b, h, 0, 0))
    grid_spec = pltpu.PrefetchScalarGridSpec(
        num_scalar_prefetch=2,
        grid=(n_batch, n_heads),
        in_specs=[col(qo), col(ko), col(vo), col(go), tab, tab, st, st],
        out_specs=[pl.BlockSpec((t, dh), lambda b, h, *_: (b, h)), st, st],
        scratch_shapes=[
            pltpu.VMEM((t, dh), F32), pltpu.VMEM((t, dh), BF16), pltpu.VMEM((t, dh), F32),
            pltpu.VMEM((dh, dh), F32), pltpu.VMEM((dh, dh), F32),
        ],
    )
    st_shape = jax.ShapeDtypeStruct((n_batch, n_heads, dh, dh), F32)
    return pl.pallas_call(
        kern,
        grid_spec=grid_spec,
        out_shape=[jax.ShapeDtypeStruct((n_batch * t, rw), BF16), st_shape, st_shape],
        compiler_params=_params("parallel", "parallel"),
        name="retention",
    )(lgf, lgb, u, u, u, u, cos, sin, s0f, s0b)


def _ctx_state_kernel(lgf_ref, lgb_ref, k_ref, v_ref, stf_ref, stb_ref, *, layer, n_heads):
    h = pl.program_id(1)
    t = k_ref.shape[0]
    lgf = lgf_ref[layer * n_heads + h]
    lgb = lgb_ref[layer * n_heads + h]
    pos = lax.broadcasted_iota(jnp.int32, (t, 1), 0).astype(F32)
    k = k_ref[...]
    vb = v_ref[...].astype(BF16)
    stf_ref[...] = _dot_tn((k * jnp.exp(lgf * (t - 1.0 - pos))).astype(BF16), vb)
    stb_ref[...] = _dot_tn((k * jnp.exp(lgb * pos)).astype(BF16), vb)


def ctx_states(lgf, lgb, kv, layer, n_batch, t, n_heads, dh):
    kern = functools.partial(_ctx_state_kernel, layer=layer, n_heads=n_heads)
    st = pl.BlockSpec((None, None, dh, dh), lambda b, h, *_: (b, h, 0, 0))
    grid_spec = pltpu.PrefetchScalarGridSpec(
        num_scalar_prefetch=2,
        grid=(n_batch, n_heads),
        in_specs=[
            pl.BlockSpec((t, dh), lambda b, h, *_: (b, h)),
            pl.BlockSpec((t, dh), lambda b, h, *_: (b, n_heads + h)),
        ],
        out_specs=[st, st],
    )
    st_shape = jax.ShapeDtypeStruct((n_batch, n_heads, dh, dh), F32)
    return pl.pallas_call(
        kern,
        grid_spec=grid_spec,
        out_shape=[st_shape, st_shape],
        compiler_params=_params("parallel", "parallel"),
        name="ctx_states",
    )(lgf, lgb, kv, kv)


def _row_copy(src_ref, src_row, dst_ref, dst_row, sem):
    return pltpu.make_async_copy(src_ref.at[pl.ds(src_row, 1), :], dst_ref.at[pl.ds(dst_row, 1), :], sem)


def _rows_done(like_ref, n_rows, sem):
    span = like_ref.at[pl.ds(0, n_rows), :]
    return pltpu.make_async_copy(span, span, sem)


def _dispatch_kernel(slot_ref, ztile_ref, h_ref, xs_ref, zero_ref, sem_ref, zsem_ref, *, n_rows, n_exp, bmx):
    i = pl.program_id(0)
    tt = h_ref.shape[0]

    n_tiles = xs_ref.shape[0] // bmx

    def zero_copy(tile):
        start = pl.multiple_of(tile * bmx, bmx)
        return pltpu.make_async_copy(zero_ref, xs_ref.at[pl.ds(start, bmx), :], zsem_ref)

    @pl.when(i == 0)
    def _():
        zero_ref[...] = jnp.zeros_like(zero_ref)
        n_used = ztile_ref[n_exp]

        def start(e, carry):
            @pl.when(ztile_ref[e] >= 0)
            def _():
                zero_copy(ztile_ref[e]).start()
            return carry

        def wait(e, carry):
            @pl.when(ztile_ref[e] >= 0)
            def _():
                zero_copy(ztile_ref[e]).wait()
            return carry

        def start_tail(j, carry):
            zero_copy(j).start()
            return carry

        def wait_tail(j, carry):
            zero_copy(j).wait()
            return carry

        lax.fori_loop(0, n_exp, start, 0)
        lax.fori_loop(n_used, n_tiles, start_tail, 0)
        lax.fori_loop(0, n_exp, wait, 0)
        lax.fori_loop(n_used, n_tiles, wait_tail, 0)

    def issue(t, carry):
        for k in range(TOP_K):
            _row_copy(h_ref, t, xs_ref, slot_ref[k * n_rows + i * tt + t], sem_ref).start()
        return carry

    lax.fori_loop(0, tt, issue, 0)
    _rows_done(xs_ref, TOP_K * tt, sem_ref).wait()


def moe_dispatch(slots, ztile, h, n_slots, bmx):
    rows, d = h.shape
    n_exp = ztile.shape[0] - 1
    tt = ROW_TILE
    kern = functools.partial(_dispatch_kernel, n_rows=rows, n_exp=n_exp, bmx=bmx)
    grid_spec = pltpu.PrefetchScalarGridSpec(
        num_scalar_prefetch=2,
        grid=(rows // tt,),
        in_specs=[pl.BlockSpec((tt, d), lambda i, *_: (i, 0))],
        out_specs=pl.BlockSpec(memory_space=pl.ANY),
        scratch_shapes=[pltpu.VMEM((bmx, d), h.dtype), pltpu.SemaphoreType.DMA(()), pltpu.SemaphoreType.DMA(())],
    )
    return pl.pallas_call(
        kern,
        grid_spec=grid_spec,
        out_shape=jax.ShapeDtypeStruct((n_slots, d), h.dtype),
        compiler_params=_params("arbitrary"),
        name="moe_dispatch",
    )(slots, ztile, h)


def _experts_kernel(texp_ref, nused_ref, x_ref, w1_ref, w3_ref, w2_ref, o_ref, w1b_ref, w3b_ref, w2b_ref):
    i = pl.program_id(0)
    valid = i < nused_ref[0]
    fresh = (i == 0) | (valid & (texp_ref[i] != texp_ref[jnp.maximum(i - 1, 0)]))

    @pl.when(fresh)
    def _():
        w1b_ref[...] = w1_ref[...].astype(BF16)
        w3b_ref[...] = w3_ref[...].astype(BF16)
        w2b_ref[...] = w2_ref[...].astype(BF16)

    @pl.when(valid)
    def _():
        x = _unpack_bf16(x_ref[...])
        a = _silu(_dot(x, w1b_ref[...])) * _dot(x, w3b_ref[...])
        o_ref[...] = _pack_halves(_dot(a.astype(BF16), w2b_ref[...]))

    @pl.when(jnp.logical_not(valid))
    def _():
        o_ref[...] = jnp.zeros_like(o_ref)


def moe_experts(tile_expert, n_used, xs, w1, w3, w2, layer, bmx):
    n_slots, dp = xs.shape
    d, hid = w1.shape[2], w1.shape[3]
    tile = lambda i, texp, nused: jnp.minimum(i, nused[0] - 1)
    rows = pl.BlockSpec((bmx, dp), lambda i, texp, nused: (tile(i, texp, nused), 0))
    out_rows = pl.BlockSpec((bmx, dp), lambda i, texp, nused: (i, 0))
    up = pl.BlockSpec((None, None, d, hid), lambda i, texp, nused: (layer, texp[tile(i, texp, nused)], 0, 0))
    down = pl.BlockSpec((None, None, hid, d), lambda i, texp, nused: (layer, texp[tile(i, texp, nused)], 0, 0))
    grid_spec = pltpu.PrefetchScalarGridSpec(
        num_scalar_prefetch=2,
        grid=(n_slots // bmx,),
        in_specs=[rows, up, up, down],
        out_specs=out_rows,
        scratch_shapes=[pltpu.VMEM((d, hid), BF16), pltpu.VMEM((d, hid), BF16), pltpu.VMEM((hid, d), BF16)],
    )
    return pl.pallas_call(
        _experts_kernel,
        grid_spec=grid_spec,
        out_shape=jax.ShapeDtypeStruct((n_slots, dp), xs.dtype),
        compiler_params=_params("arbitrary"),
        name="moe_experts",
    )(tile_expert, n_used, xs, w1, w3, w2)


def _combine_kernel(slot_ref, w_ref, ys_ref, o_ref, buf_ref, sem_ref, *, n_rows):
    i = pl.program_id(0)
    n = pl.num_programs(0)
    tt = o_ref.shape[0]

    def issue(tile, buf):
        def body(t, carry):
            for k in range(TOP_K):
                _row_copy(ys_ref, slot_ref[k * n_rows + tile * tt + t], buf_ref.at[buf], k * tt + t,
                          sem_ref.at[buf]).start()
            return carry

        lax.fori_loop(0, tt, body, 0)

    @pl.when(i == 0)
    def _():
        issue(0, 0)

    @pl.when(i + 1 < n)
    def _():
        issue(i + 1, (i + 1) % 2)

    buf = i % 2
    _rows_done(ys_ref, TOP_K * tt, sem_ref.at[buf]).wait()
    w = w_ref[...]
    half = ys_ref.shape[1]
    acc_lo = jnp.zeros((tt, half), F32)
    acc_hi = jnp.zeros((tt, half), F32)
    for k in range(TOP_K):
        lo, hi = _unpack_halves(buf_ref[buf, k * tt:(k + 1) * tt, :])
        acc_lo = acc_lo + lo * w[:, k:k + 1]
        acc_hi = acc_hi + hi * w[:, k:k + 1]
    o_ref[:, :half] = acc_lo
    o_ref[:, half:] = acc_hi


def moe_combine(slots, wts, ys, rows):
    dp = ys.shape[1]
    tt = COMBINE_TT
    kern = functools.partial(_combine_kernel, n_rows=rows)
    grid_spec = pltpu.PrefetchScalarGridSpec(
        num_scalar_prefetch=1,
        grid=(rows // tt,),
        in_specs=[pl.BlockSpec((tt, TOP_K), lambda i, *_: (i, 0)), pl.BlockSpec(memory_space=pl.ANY)],
        out_specs=pl.BlockSpec((tt, 2 * dp), lambda i, *_: (i, 0)),
        scratch_shapes=[pltpu.VMEM((2, TOP_K * tt, dp), ys.dtype), pltpu.SemaphoreType.DMA((2,))],
    )
    return pl.pallas_call(
        kern,
        grid_spec=grid_spec,
        out_shape=jax.ShapeDtypeStruct((rows, 2 * dp), F32),
        compiler_params=_params("arbitrary"),
        name="moe_combine",
    )(slots, wts, ys)


def routing_tables(eidx, pos, counts, bmx, n_tiles):
    n_exp = counts.shape[0]
    padded = (counts + bmx - 1) // bmx * bmx
    ends = jnp.cumsum(padded)
    starts = ends - padded
    experts = jnp.arange(n_exp, dtype=jnp.int32)[:, None, None]
    slots = (jnp.sum(jnp.where(eidx[None] == experts, starts[:, None, None], 0), axis=0) + pos).reshape(-1)
    tile_ends = ends // bmx
    n_used = tile_ends[-1:]
    tiles = jnp.arange(n_tiles, dtype=jnp.int32)
    tile_expert = jnp.sum((tile_ends[None, :] <= tiles[:, None]).astype(jnp.int32), axis=1)
    tile_expert = jnp.minimum(tile_expert, n_exp - 1).astype(jnp.int32)
    ztile = jnp.concatenate([jnp.where(counts > 0, tile_ends - 1, -1), n_used]).astype(jnp.int32)
    return slots.astype(jnp.int32), ztile, tile_expert, n_used.astype(jnp.int32)


def _fused_experts_kernel(texp_ref, nused_ref, tok_ref, row_ref, h_ref, w1_ref, w3_ref, w2_ref, ys_ref,
                          w1b_ref, w3b_ref, w2b_ref, xbuf0, xbuf1, ybuf0, ybuf1, gsem, ssem, zsem,
                          *, n_pairs, bmx):
    i = pl.program_id(0)
    n_steps = pl.num_programs(0)
    n_used = nused_ref[0]
    valid = i < n_used
    xbufs, ybufs = (xbuf0, xbuf1), (ybuf0, ybuf1)

    def gather(tile, xbuf, sem):
        for r in range(bmx):
            pltpu.make_async_copy(h_ref.at[pl.ds(tok_ref[tile + 1, r], 1), :], xbuf.at[pl.ds(r, 1), :], sem).start()

    def scatter(tile, ybuf, sem):
        for r in range(bmx):
            pltpu.make_async_copy(ybuf.at[pl.ds(r, 1), :], ys_ref.at[pl.ds(row_ref[tile + 1, r], 1), :], sem).start()

    def tile_done(sem):
        return _rows_done(ys_ref, bmx, sem)

    @pl.when(i == 0)
    def _():
        ybuf1[...] = jnp.zeros_like(ybuf1)
        n_fill = (ys_ref.shape[0] - n_pairs) // bmx
        for j in range(n_fill):
            pltpu.make_async_copy(ybuf1, ys_ref.at[pl.ds(n_pairs + j * bmx, bmx), :], zsem).start()
        for j in range(n_fill):
            pltpu.make_async_copy(ybuf1, ys_ref.at[pl.ds(n_pairs + j * bmx, bmx), :], zsem).wait()
        gather(0, xbuf0, gsem.at[0])

    fresh = (i == 0) | (valid & (texp_ref[i] != texp_ref[jnp.maximum(i - 1, 0)]))

    @pl.when(fresh)
    def _():
        w1b_ref[...] = w1_ref[...].astype(BF16)
        w3b_ref[...] = w3_ref[...].astype(BF16)
        w2b_ref[...] = w2_ref[...].astype(BF16)

    for p in range(2):
        q = 1 - p

        @pl.when(valid & (i % 2 == p))
        def _():
            tile_done(gsem.at[p]).wait()
            gather(i + 1, xbufs[q], gsem.at[q])
            scatter(i - 1, ybufs[q], ssem.at[q])
            x = _unpack_bf16(xbufs[p][...])
            a = _silu(_dot(x, w1b_ref[...])) * _dot(x, w3b_ref[...])
            ybufs[p][...] = _pack_halves(_dot(a.astype(BF16), w2b_ref[...]))
            tile_done(ssem.at[q]).wait()

        @pl.when((i == n_used) & (i % 2 == p))
        def _():
            tile_done(gsem.at[p]).wait()
            scatter(i - 1, ybufs[q], ssem.at[q])
            tile_done(ssem.at[q]).wait()

    for p in range(2):
        q = 1 - p

        @pl.when((i == n_steps - 1) & valid & (i % 2 == p))
        def _():
            tile_done(gsem.at[q]).wait()
            scatter(i, ybufs[p], ssem.at[p])
            tile_done(ssem.at[p]).wait()


def moe_experts_fused(tile_expert, n_used, toks, rows, h, w1, w3, w2, layer, n_tiles, bmx):
    n_tok, dp = h.shape
    d, hid = w1.shape[2], w1.shape[3]
    n_pairs = TOP_K * n_tok
    n_rows = (n_tiles + 1) * bmx
    expert = lambda i, texp, nused, *_: (layer, texp[jnp.minimum(i, nused[0] - 1)], 0, 0)
    up = pl.BlockSpec((None, None, d, hid), expert)
    down = pl.BlockSpec((None, None, hid, d), expert)
    kern = functools.partial(_fused_experts_kernel, n_pairs=n_pairs, bmx=bmx)
    grid_spec = pltpu.PrefetchScalarGridSpec(
        num_scalar_prefetch=4,
        grid=(n_tiles,),
        in_specs=[pl.BlockSpec(memory_space=pl.ANY), up, up, down],
        out_specs=pl.BlockSpec(memory_space=pl.ANY),
        scratch_shapes=[
            pltpu.VMEM((d, hid), BF16), pltpu.VMEM((d, hid), BF16), pltpu.VMEM((hid, d), BF16),
            pltpu.VMEM((bmx, dp), h.dtype), pltpu.VMEM((bmx, dp), h.dtype),
            pltpu.VMEM((bmx, dp), h.dtype), pltpu.VMEM((bmx, dp), h.dtype),
            pltpu.SemaphoreType.DMA((2,)), pltpu.SemaphoreType.DMA((2,)), pltpu.SemaphoreType.DMA(()),
        ],
    )
    return pl.pallas_call(
        kern,
        grid_spec=grid_spec,
        out_shape=jax.ShapeDtypeStruct((n_rows, dp), h.dtype),
        compiler_params=_params("arbitrary"),
        name="moe_experts_fused",
    )(tile_expert, n_used, toks, rows, h, w1, w3, w2)


def _dense_combine_kernel(w_ref, *refs):
    y_refs, o_ref = refs[:TOP_K], refs[TOP_K]
    tt = o_ref.shape[0]
    half = y_refs[0].shape[1]
    w = w_ref[...]
    acc_lo = jnp.zeros((tt, half), F32)
    acc_hi = jnp.zeros((tt, half), F32)
    for k in range(TOP_K):
        lo, hi = _unpack_halves(y_refs[k][...])
        acc_lo = acc_lo + lo * w[:, k:k + 1]
        acc_hi = acc_hi + hi * w[:, k:k + 1]
    o_ref[:, :half] = acc_lo
    o_ref[:, half:] = acc_hi


def moe_dense_combine(wts, ys, rows):
    dp = ys.shape[1]
    tt = COMBINE_TT
    per_k = rows // tt
    y_specs = [pl.BlockSpec((tt, dp), functools.partial(lambda i, k: (k * per_k + i, 0), k=k)) for k in range(TOP_K)]
    return pl.pallas_call(
        _dense_combine_kernel,
        grid=(rows // tt,),
        in_specs=[pl.BlockSpec((tt, TOP_K), lambda i: (i, 0))] + y_specs,
        out_specs=pl.BlockSpec((tt, 2 * dp), lambda i: (i, 0)),
        out_shape=jax.ShapeDtypeStruct((rows, 2 * dp), F32),
        compiler_params=_params("parallel"),
        name="moe_dense_combine",
    )(wts, *([ys] * TOP_K))


def fused_routing_tables(eidx, pos, counts, bmx, n_tiles):
    n_exp = counts.shape[0]
    n_tok = eidx.shape[1]
    n_pairs = TOP_K * n_tok
    n_slots = n_tiles * bmx
    n_pad = n_slots - n_pairs
    padded = (counts + bmx - 1) // bmx * bmx
    ends = jnp.cumsum(padded)
    starts = ends - padded
    experts = jnp.arange(n_exp, dtype=jnp.int32)
    slots = jnp.sum(jnp.where(eidx[None] == experts[:, None, None], starts[:, None, None], 0), axis=0) + pos
    pad_counts = jnp.concatenate([padded - counts, n_slots - ends[-1:]])
    pad_first = jnp.concatenate([starts + counts, ends[-1:]])
    pad_ends = jnp.cumsum(pad_counts)
    j = jnp.arange(n_pad, dtype=jnp.int32)
    owner = jnp.sum((pad_ends[None, :] <= j[:, None]).astype(jnp.int32), axis=1)
    owners = jnp.arange(n_exp + 1, dtype=jnp.int32)
    base = jnp.sum(jnp.where(owner[:, None] == owners[None, :], (pad_first - pad_ends + pad_counts)[None, :], 0), axis=1)
    pad_slot = base + j
    keys = jnp.concatenate([slots.reshape(-1), pad_slot]).astype(jnp.int32)
    rows = jnp.arange(n_slots, dtype=jnp.int32)
    toks = jnp.concatenate([jnp.tile(jnp.arange(n_tok, dtype=jnp.int32), TOP_K), jnp.zeros((n_pad,), jnp.int32)])
    _, rows, toks = lax.sort((keys, rows, toks), num_keys=1)
    spare = n_slots + jnp.arange(bmx, dtype=jnp.int32)
    rows = jnp.concatenate([spare, rows, spare]).reshape(n_tiles + 2, bmx)
    toks = jnp.concatenate([jnp.zeros_like(spare), toks, jnp.zeros_like(spare)]).reshape(n_tiles + 2, bmx)
    tile_ends = ends // bmx
    n_used = tile_ends[-1:].astype(jnp.int32)
    tiles = jnp.arange(n_tiles, dtype=jnp.int32)
    tile_expert = jnp.sum((tile_ends[None, :] <= tiles[:, None]).astype(jnp.int32), axis=1)
    tile_expert = jnp.minimum(tile_expert, n_exp - 1).astype(jnp.int32)
    return toks, rows, tile_expert, n_used


def _glu_mm_kernel(h_ref, w1_ref, w3_ref, o_ref, w1b_ref, w3b_ref):
    @pl.when(pl.program_id(1) == 0)
    def _():
        w1b_ref[...] = w1_ref[...].astype(BF16)
        w3b_ref[...] = w3_ref[...].astype(BF16)

    h = _unpack_bf16(h_ref[...])
    o_ref[...] = (_silu(_dot(h, w1b_ref[...])) * _dot(h, w3b_ref[...])).astype(o_ref.dtype)


def glu_matmul(h, w1, w3, layer, rows, bm):
    k = w1.shape[1]
    n = w1.shape[2]
    bn = HID_CHUNK
    return pl.pallas_call(
        _glu_mm_kernel,
        grid=(n // bn, rows // bm),
        in_specs=[
            pl.BlockSpec((bm, h.shape[1]), lambda j, i: (i, 0)),
            pl.BlockSpec((None, k, bn), lambda j, i: (layer, 0, j)),
            pl.BlockSpec((None, k, bn), lambda j, i: (layer, 0, j)),
        ],
        out_specs=pl.BlockSpec((bm, bn), lambda j, i: (i, j)),
        out_shape=jax.ShapeDtypeStruct((rows, n), BF16),
        scratch_shapes=[pltpu.VMEM((k, bn), BF16), pltpu.VMEM((k, bn), BF16)],
        compiler_params=_params("arbitrary", "arbitrary"),
        name="glu_matmul",
    )(h, w1, w3)


def _rope_tables(seq, dh):
    rows = seq // GRID_W
    row = jnp.repeat(jnp.arange(rows, dtype=F32), GRID_W)
    col = jnp.tile(jnp.arange(GRID_W, dtype=F32), rows)
    quarter = dh // 4
    inv_freq = ROPE_BASE ** (-jnp.arange(quarter, dtype=F32) / quarter)
    ang_r = row[:, None] * inv_freq[None, :]
    ang_c = col[:, None] * inv_freq[None, :]
    cos = jnp.concatenate([jnp.cos(ang_r), jnp.cos(ang_r), jnp.cos(ang_c), jnp.cos(ang_c)], axis=-1)
    sin = jnp.concatenate([-jnp.sin(ang_r), jnp.sin(ang_r), -jnp.sin(ang_c), jnp.sin(ang_c)], axis=-1)
    return cos, sin


def kernel(x, c, ctx, c_ctx, mod_w, mod_b, norm1_g, norm2_g, w_in, conv_w, conv_b, conv_ln_g, conv_ln_b,
           w_conv_out, ret_log_gamma_fwd, ret_log_gamma_bwd, w_ret_out, w_merge_out, router_w, router_bias,
           exp_w1, exp_w3, exp_w2, shared_w1, shared_w3, shared_w2, final_g):
    b, s, d = x.shape
    n_ctx = ctx.shape[1]
    depth = mod_w.shape[0]
    cc = conv_w.shape[2]
    n_heads = ret_log_gamma_fwd.shape[1]
    rw = w_ret_out.shape[1]
    dh = rw // n_heads
    n_exp = router_w.shape[2]
    n_lat, n_cx = b * s, b * n_ctx
    n_all = n_lat + n_cx
    bm = min(MM_BM, s, n_cx)
    assert b + 1 <= 8 and s % ROW_TILE == 0 and n_ctx % ROW_TILE == 0 and dh == 2 * V7X_LANES
    assert s % bm == 0 and n_cx % bm == 0 and s % MOE_TM == 0 and n_cx % MOE_TM == 0
    q_off, gate_off = 2 * cc, 2 * cc + 4 * rw

    xa = jnp.concatenate([x.reshape(n_lat, d), ctx.reshape(n_cx, d)], axis=0)
    c8 = jnp.zeros((8, d), F32).at[:b].set(c).at[b].set(c_ctx)
    mod_b3 = mod_b.reshape(depth, 1, -1)
    n1g, n2g = norm1_g.reshape(depth, 1, d), norm2_g.reshape(depth, 1, d)
    conv_b3, ln_g3, ln_b3 = (a.reshape(depth, 1, cc) for a in (conv_b, conv_ln_g, conv_ln_b))
    lgf, lgb = ret_log_gamma_fwd.reshape(-1), ret_log_gamma_bwd.reshape(-1)
    rw_t = jnp.swapaxes(router_w, 1, 2)
    rw_hi = rw_t.astype(BF16)
    rw_lo = (rw_t - rw_hi.astype(F32)).astype(BF16)
    rw_cat = jnp.concatenate([rw_hi, rw_lo], axis=1)
    bias3 = router_bias.reshape(depth, n_exp, 1)
    cos, sin = _rope_tables(s, dh)
    zero_st = jnp.zeros((b, n_heads, dh, dh), F32)

    for layer in range(depth):
        last = layer == depth - 1
        rows = n_lat if last else n_all
        seq_args = dict(n_lat_rows=n_lat, seq=s, n_batch=b)
        mod3 = modulation(c8, mod_w, mod_b3, layer).reshape(8 * N_MOD, 1, d)
        h1 = norm_mod(xa, n1g, mod3, layer, 0, 1, n_all, **seq_args)
        if last:
            u = matmul(h1, w_in, layer, n_lat, 0, 0, w_in.shape[2], bm)
            kv_c = matmul(h1, w_in, layer, n_cx, n_lat, q_off + rw, 2 * rw, bm)
            st_f, st_b = ctx_states(lgf, lgb, kv_c, layer, b, n_ctx, n_heads, dh)
        else:
            u = matmul(h1, w_in, layer, n_all, 0, 0, w_in.shape[2], bm)
            yr_c, st_f, st_b = retention(lgf, lgb, u, cos, sin, zero_st, zero_st, layer, b, n_ctx, n_lat, q_off,
                                         n_heads, dh, rope=False, zero_init=True)
        yc = conv_branch(u, conv_w, conv_b3, ln_g3, ln_b3, layer, rows, n_lat, s, n_ctx)
        yr, _, _ = retention(lgf, lgb, u, cos, sin, st_f, st_b, layer, b, s, 0, q_off, n_heads, dh,
                             rope=True, zero_init=False)
        if not last:
            yr = jnp.concatenate([yr, yr_c], axis=0)
        mixed = merge_branches(yc, yr, u, w_conv_out, w_ret_out, layer, rows, gate_off, bm)
        xa = resid_matmul(mixed, w_merge_out, xa, mod3, layer, 2, rows, bm=bm, **seq_args)
        h2, eidx, pos, wts, cnt = norm_router(xa, n2g, mod3, rw_cat, rw_hi, bias3, layer, 3, 4, rows, **seq_args)
        n_tiles = TOP_K * rows // MOE_BMX + n_exp
        toks, prow, tile_expert, n_used = fused_routing_tables(eidx, pos, cnt[:, 0].astype(jnp.int32), MOE_BMX, n_tiles)
        ys = moe_experts_fused(tile_expert, n_used, toks, prow, h2, exp_w1, exp_w3, exp_w2, layer, n_tiles, MOE_BMX)
        routed = moe_dense_combine(wts.T, ys, rows)
        act = glu_matmul(h2, shared_w1, shared_w3, layer, rows, MOE_TM)
        xa = resid_matmul(act, shared_w2, xa, mod3, layer, 5, rows, bm=bm, extra=routed, **seq_args)
    out = final_norm(xa, final_g.reshape(1, d), n_lat)
    return out.reshape(b, s, d)
```

```python
import functools

import jax
import jax.numpy as jnp
from jax import lax
from jax.experimental import pallas as pl
from jax.experimental.pallas import tpu as pltpu

GRID_W = 64
N_GROUPS = 8
TOPK_GROUPS = 4
TOP_K = 8
ROUTED_SCALE = 2.5
N_MOD = 6
NORM_EPS = 1e-6
ROPE_BASE = 10000.0

V7X_LANES = 128
V7X_SUBLANES = 8
V7X_VMEM_BYTES = 64 * 1024 * 1024
VMEM_LIMIT = V7X_VMEM_BYTES - 8 * 1024 * 1024

ROW_TILE = 256
HALO = 16
MM_BM = 1024
MM_BN = 512
MOE_TM = 512
MOE_BMX = 256
COMBINE_TT = 128
COPY_GROUP = 32
ROW_BITS = 17
HID_CHUNK = 256

BF16 = jnp.bfloat16
F32 = jnp.float32


def _params(*sem):
    return pltpu.CompilerParams(dimension_semantics=sem, vmem_limit_bytes=VMEM_LIMIT)


def _dot(a, b):
    return jnp.dot(a, b, preferred_element_type=F32)


def _dot_nt(a, b):
    return lax.dot_general(a, b, (((1,), (1,)), ((), ())), preferred_element_type=F32)


def _dot_tn(a, b):
    return lax.dot_general(a, b, (((0,), (0,)), ((), ())), preferred_element_type=F32)


def _silu(x):
    return x * jax.nn.sigmoid(x)


def _pack_halves(v):
    half = v.shape[1] // 2
    lo = pltpu.bitcast(v[:, :half].astype(BF16).astype(F32), jnp.uint32)
    hi = pltpu.bitcast(v[:, half:].astype(BF16).astype(F32), jnp.uint32)
    return (hi & jnp.uint32(0xFFFF0000)) | (lo >> 16)


def _unpack_halves(p):
    lo = pltpu.bitcast(p << 16, F32)
    hi = pltpu.bitcast(p & jnp.uint32(0xFFFF0000), F32)
    return lo, hi


def _unpack_bf16(p):
    lo, hi = _unpack_halves(p)
    return jnp.concatenate([lo.astype(BF16), hi.astype(BF16)], axis=1)


def _mod_row(i, n_lat_tiles, tiles_per_seq, n_batch):
    return jnp.where(i < n_lat_tiles, i // tiles_per_seq, n_batch)


def _mod_kernel(c_ref, w_ref, b_ref, o_ref):
    c = c_ref[...]
    sc = _silu(c).astype(BF16)
    o_ref[...] = _dot(sc, w_ref[...].astype(BF16)) + b_ref[...]


def modulation(c8, mod_w, mod_b3, layer):
    d = c8.shape[1]
    n = mod_w.shape[2]
    bn = MM_BN
    return pl.pallas_call(
        _mod_kernel,
        grid=(n // bn,),
        in_specs=[
            pl.BlockSpec((8, d), lambda j: (0, 0)),
            pl.BlockSpec((None, d, bn), lambda j: (layer, 0, j)),
            pl.BlockSpec((None, 1, bn), lambda j: (layer, 0, j)),
        ],
        out_specs=pl.BlockSpec((8, bn), lambda j: (0, j)),
        out_shape=jax.ShapeDtypeStruct((8, n), F32),
        compiler_params=_params("arbitrary"),
        name="modulation",
    )(c8, mod_w, mod_b3)


def _rms(x, g):
    return x * lax.rsqrt(jnp.mean(x * x, axis=-1, keepdims=True) + NORM_EPS) * g


def _norm_mod_kernel(x_ref, g_ref, sh_ref, sc_ref, o_ref):
    h = _rms(x_ref[...], g_ref[...]) * (1.0 + sc_ref[...]) + sh_ref[...]
    o_ref[...] = h.astype(o_ref.dtype)


def norm_mod(x, g3, mod3, layer, sh_idx, sc_idx, rows, n_lat_rows, seq, n_batch):
    d = x.shape[1]
    tr = ROW_TILE
    row = functools.partial(_mod_row, n_lat_tiles=n_lat_rows // tr, tiles_per_seq=seq // tr, n_batch=n_batch)
    return pl.pallas_call(
        _norm_mod_kernel,
        grid=(rows // tr,),
        in_specs=[
            pl.BlockSpec((tr, d), lambda i: (i, 0)),
            pl.BlockSpec((None, 1, d), lambda i: (layer, 0, 0)),
            pl.BlockSpec((None, 1, d), lambda i: (row(i) * N_MOD + sh_idx, 0, 0)),
            pl.BlockSpec((None, 1, d), lambda i: (row(i) * N_MOD + sc_idx, 0, 0)),
        ],
        out_specs=pl.BlockSpec((tr, d), lambda i: (i, 0)),
        out_shape=jax.ShapeDtypeStruct((rows, d), BF16),
        compiler_params=_params("parallel"),
        name="norm_mod",
    )(x, g3, mod3, mod3)


def _final_norm_kernel(x_ref, g_ref, o_ref):
    o_ref[...] = _rms(x_ref[...], g_ref[...])


def final_norm(x, g2, rows):
    d = x.shape[1]
    tr = ROW_TILE
    return pl.pallas_call(
        _final_norm_kernel,
        grid=(rows // tr,),
        in_specs=[pl.BlockSpec((tr, d), lambda i: (i, 0)), pl.BlockSpec((1, d), lambda i: (0, 0))],
        out_specs=pl.BlockSpec((tr, d), lambda i: (i, 0)),
        out_shape=jax.ShapeDtypeStruct((rows, d), F32),
        compiler_params=_params("parallel"),
        name="final_norm",
    )(x, g2)


def _route(logits_t, bias):
    n_exp, tt = logits_t.shape
    per = n_exp // N_GROUPS
    s = jax.nn.sigmoid(logits_t)
    biased = s + bias
    g3 = biased.reshape(N_GROUPS, per, tt)
    mem = lax.broadcasted_iota(jnp.int32, g3.shape, 1)
    m1 = jnp.max(g3, axis=1, keepdims=True)
    first = jnp.min(jnp.where(g3 == m1, mem, per), axis=1, keepdims=True)
    m2 = jnp.max(jnp.where(mem == first, -jnp.inf, g3), axis=1, keepdims=True)
    gs = m1 + m2
    gid = lax.broadcasted_iota(jnp.int32, gs.shape, 0)
    grank = jnp.zeros(gs.shape, jnp.int32)
    for j in range(N_GROUPS):
        other = gs[j:j + 1]
        ahead = (other > gs) | ((other == gs) & (gid > j))
        grank = grank + ahead.astype(jnp.int32)
    gmask = grank < TOPK_GROUPS
    masked = jnp.where(gmask, g3, -jnp.inf).reshape(n_exp, tt)
    eid = lax.broadcasted_iota(jnp.int32, masked.shape, 0)
    rank = jnp.zeros(masked.shape, jnp.int32)
    for j in range(n_exp):
        other = masked[j:j + 1, :]
        ahead = (other > masked) | ((other == masked) & (eid > j))
        rank = rank + ahead.astype(jnp.int32)
    sel = jnp.where(rank < TOP_K, s, 0.0)
    return sel / jnp.sum(sel, axis=0, keepdims=True) * ROUTED_SCALE, rank


def _norm_router_kernel(x_ref, g_ref, sh_ref, sc_ref, wcat_ref, whi_ref, bias_ref,
                        h_ref, eidx_ref, pos_ref, wts_ref, cnt_ref):
    i = pl.program_id(0)
    h = _rms(x_ref[...], g_ref[...]) * (1.0 + sc_ref[...]) + sh_ref[...]
    h_hi = h.astype(BF16)
    h_lo = (h - h_hi.astype(F32)).astype(BF16)
    h_ref[...] = _pack_halves(h)
    n_exp = whi_ref.shape[0]
    lt = _dot_nt(wcat_ref[...], h_hi)
    logits_t = lt[:n_exp] + lt[n_exp:] + _dot_nt(whi_ref[...], h_lo)
    comb, rank = _route(logits_t, bias_ref[...])
    tt = comb.shape[1]

    @pl.when(i == 0)
    def _():
        cnt_ref[...] = jnp.zeros_like(cnt_ref)

    chosen = (rank < TOP_K).astype(F32)
    ta = lax.broadcasted_iota(jnp.int32, (tt, tt), 0)
    tb = lax.broadcasted_iota(jnp.int32, (tt, tt), 1)
    before = (ta < tb).astype(BF16)
    prefix = _dot(chosen.astype(BF16), before)
    base = cnt_ref[:, 0:1]
    pos = base + prefix
    cnt_ref[...] = cnt_ref[...] + jnp.sum(chosen, axis=1, keepdims=True)
    eid = lax.broadcasted_iota(jnp.int32, comb.shape, 0).astype(F32)
    for k in range(TOP_K):
        mk = rank == k
        eidx_ref[k:k + 1, :] = jnp.sum(jnp.where(mk, eid, 0.0), axis=0, keepdims=True).astype(jnp.int32)
        pos_ref[k:k + 1, :] = jnp.sum(jnp.where(mk, pos, 0.0), axis=0, keepdims=True).astype(jnp.int32)
        wts_ref[k:k + 1, :] = jnp.sum(jnp.where(mk, comb, 0.0), axis=0, keepdims=True)


def norm_router(x, g3, mod3, wcat_t, whi_t, bias3, layer, sh_idx, sc_idx, rows, n_lat_rows, seq, n_batch):
    d = x.shape[1]
    n_exp = whi_t.shape[1]
    tr = ROW_TILE
    row = functools.partial(_mod_row, n_lat_tiles=n_lat_rows // tr, tiles_per_seq=seq // tr, n_batch=n_batch)
    lists = pl.BlockSpec((TOP_K, tr), lambda i: (0, i))
    return pl.pallas_call(
        _norm_router_kernel,
        grid=(rows // tr,),
        in_specs=[
            pl.BlockSpec((tr, d), lambda i: (i, 0)),
            pl.BlockSpec((None, 1, d), lambda i: (layer, 0, 0)),
            pl.BlockSpec((None, 1, d), lambda i: (row(i) * N_MOD + sh_idx, 0, 0)),
            pl.BlockSpec((None, 1, d), lambda i: (row(i) * N_MOD + sc_idx, 0, 0)),
            pl.BlockSpec((None, 2 * n_exp, d), lambda i: (layer, 0, 0)),
            pl.BlockSpec((None, n_exp, d), lambda i: (layer, 0, 0)),
            pl.BlockSpec((None, n_exp, 1), lambda i: (layer, 0, 0)),
        ],
        out_specs=[
            pl.BlockSpec((tr, d // 2), lambda i: (i, 0)),
            lists, lists, lists,
            pl.BlockSpec((n_exp, V7X_LANES), lambda i: (0, 0)),
        ],
        out_shape=[
            jax.ShapeDtypeStruct((rows, d // 2), jnp.uint32),
            jax.ShapeDtypeStruct((TOP_K, rows), jnp.int32),
            jax.ShapeDtypeStruct((TOP_K, rows), jnp.int32),
            jax.ShapeDtypeStruct((TOP_K, rows), F32),
            jax.ShapeDtypeStruct((n_exp, V7X_LANES), F32),
        ],
        compiler_params=_params("arbitrary"),
        name="norm_router",
    )(x, g3, mod3, mod3, wcat_t, whi_t, bias3)


def _mm_kernel(a_ref, w_ref, o_ref, wb_ref):
    @pl.when(pl.program_id(1) == 0)
    def _():
        wb_ref[...] = w_ref[...].astype(BF16)

    o_ref[...] = _dot(a_ref[...], wb_ref[...]).astype(o_ref.dtype)


def matmul(a, w, layer, rows, row_off, col_off, ncols, bm, out_dtype=F32):
    k = a.shape[1]
    bn = MM_BN
    ro, co = row_off // bm, col_off // bn
    return pl.pallas_call(
        _mm_kernel,
        grid=(ncols // bn, rows // bm),
        in_specs=[
            pl.BlockSpec((bm, k), lambda j, i: (i + ro, 0)),
            pl.BlockSpec((None, k, bn), lambda j, i: (layer, 0, j + co)),
        ],
        out_specs=pl.BlockSpec((bm, bn), lambda j, i: (i, j)),
        out_shape=jax.ShapeDtypeStruct((rows, ncols), out_dtype),
        scratch_shapes=[pltpu.VMEM((k, bn), BF16)],
        compiler_params=_params("arbitrary", "arbitrary"),
        name="matmul",
    )(a, w)


def _merge_kernel(yc_ref, yr_ref, gc_ref, gr_ref, wc_ref, wr_ref, o_ref, wcb_ref, wrb_ref):
    @pl.when(pl.program_id(1) == 0)
    def _():
        wcb_ref[...] = wc_ref[...].astype(BF16)
        wrb_ref[...] = wr_ref[...].astype(BF16)

    y = jax.nn.sigmoid(gc_ref[...]) * _dot(yc_ref[...], wcb_ref[...])
    y = y + jax.nn.sigmoid(gr_ref[...]) * _dot(yr_ref[...], wrb_ref[...])
    o_ref[...] = y.astype(o_ref.dtype)


def merge_branches(yc, yr, u, w_conv_out, w_ret_out, layer, rows, gate_off, bm):
    kc, kr = yc.shape[1], yr.shape[1]
    d = w_conv_out.shape[2]
    bn = MM_BN
    gco, gro = gate_off // bn, (gate_off + d) // bn
    return pl.pallas_call(
        _merge_kernel,
        grid=(d // bn, rows // bm),
        in_specs=[
            pl.BlockSpec((bm, kc), lambda j, i: (i, 0)),
            pl.BlockSpec((bm, kr), lambda j, i: (i, 0)),
            pl.BlockSpec((bm, bn), lambda j, i: (i, j + gco)),
            pl.BlockSpec((bm, bn), lambda j, i: (i, j + gro)),
            pl.BlockSpec((None, kc, bn), lambda j, i: (layer, 0, j)),
            pl.BlockSpec((None, kr, bn), lambda j, i: (layer, 0, j)),
        ],
        out_specs=pl.BlockSpec((bm, bn), lambda j, i: (i, j)),
        out_shape=jax.ShapeDtypeStruct((rows, d), BF16),
        scratch_shapes=[pltpu.VMEM((kc, bn), BF16), pltpu.VMEM((kr, bn), BF16)],
        compiler_params=_params("arbitrary", "arbitrary"),
        name="merge_branches",
    )(yc, yr, u, u, w_conv_out, w_ret_out)


def _resid_mm_kernel(a_ref, w_ref, x_ref, g_ref, o_ref, wb_ref):
    @pl.when(pl.program_id(1) == 0)
    def _():
        wb_ref[...] = w_ref[...].astype(BF16)

    o_ref[...] = x_ref[...] + g_ref[...] * _dot(a_ref[...], wb_ref[...])


def _resid_mm_add_kernel(a_ref, w_ref, x_ref, g_ref, e_ref, o_ref, wb_ref):
    @pl.when(pl.program_id(1) == 0)
    def _():
        wb_ref[...] = w_ref[...].astype(BF16)

    o_ref[...] = x_ref[...] + g_ref[...] * (e_ref[...] + _dot(a_ref[...], wb_ref[...]))


def resid_matmul(a, w, x, mod3, layer, g_idx, rows, n_lat_rows, seq, n_batch, bm, extra=None):
    k = a.shape[1]
    d = w.shape[2]
    bn = MM_BN
    row = functools.partial(_mod_row, n_lat_tiles=n_lat_rows // bm, tiles_per_seq=seq // bm, n_batch=n_batch)
    tile = pl.BlockSpec((bm, bn), lambda j, i: (i, j))
    in_specs = [
        pl.BlockSpec((bm, k), lambda j, i: (i, 0)),
        pl.BlockSpec((None, k, bn), lambda j, i: (layer, 0, j)),
        tile,
        pl.BlockSpec((None, 1, bn), lambda j, i: (row(i) * N_MOD + g_idx, 0, j)),
    ]
    operands = [a, w, x, mod3]
    if extra is not None:
        in_specs.append(tile)
        operands.append(extra)
    return pl.pallas_call(
        _resid_mm_kernel if extra is None else _resid_mm_add_kernel,
        grid=(d // bn, rows // bm),
        in_specs=in_specs,
        out_specs=tile,
        out_shape=jax.ShapeDtypeStruct((rows, d), F32),
        scratch_shapes=[pltpu.VMEM((k, bn), BF16)],
        compiler_params=_params("arbitrary", "arbitrary"),
        name="resid_matmul",
    )(*operands)


def _conv_kernel(a_ref, g_ref, ap_ref, gp_ref, an_ref, gn_ref, w_ref, b_ref, lng_ref, lnb_ref, o_ref,
                 buf_ref, acc_ref, shift_ref, *, n_lat_tiles, lat_tps, ctx_tps, conv_k):
    i = pl.program_id(0)
    tr, cc = a_ref.shape
    is_lat = i < n_lat_tiles
    pos = jnp.where(is_lat, i % lat_tps, (i - n_lat_tiles) % ctx_tps)
    tps = jnp.where(is_lat, lat_tps, ctx_tps)
    keep_prev = (pos != 0).astype(F32)
    keep_next = (pos != tps - 1).astype(F32)
    buf_ref[HALO:HALO + tr, :] = a_ref[...] * jax.nn.sigmoid(g_ref[...])
    buf_ref[0:HALO, :] = ap_ref[...] * jax.nn.sigmoid(gp_ref[...]) * keep_prev
    buf_ref[HALO + tr:, :] = an_ref[...] * jax.nn.sigmoid(gn_ref[...]) * keep_next
    pad = (conv_k - 1) // 2
    rc = 128
    n_rc = tr // rc
    span = shift_ref.shape[1]

    def chunk(t, carry):
        lanes = pl.ds(pl.multiple_of(t * V7X_LANES, V7X_LANES), V7X_LANES)
        for s in range(1, V7X_SUBLANES):
            shift_ref[s - 1] = buf_ref[s:s + span, lanes]
        for r in range(n_rc):
            acc = jnp.zeros((rc, V7X_LANES), F32)
            for kk in range(conv_k):
                whole, s = divmod(HALO - pad + kk, V7X_SUBLANES)
                start = r * rc + whole * V7X_SUBLANES
                if s == 0:
                    window = buf_ref[start:start + rc, lanes]
                else:
                    window = shift_ref[s - 1, start:start + rc, :]
                acc = acc + window * w_ref[kk:kk + 1, lanes]
            acc_ref[r * rc:(r + 1) * rc, lanes] = acc
        return carry

    lax.fori_loop(0, cc // V7X_LANES, chunk, 0)
    y = acc_ref[...] + b_ref[...]
    mu = jnp.mean(y, axis=-1, keepdims=True)
    yc = y - mu
    var = jnp.mean(yc * yc, axis=-1, keepdims=True)
    z = yc * lax.rsqrt(var + NORM_EPS) * lng_ref[...] + lnb_ref[...]
    o_ref[...] = _silu(z).astype(o_ref.dtype)


def conv_branch(u, conv_w, conv_b3, ln_g3, ln_b3, layer, rows, n_lat_rows, seq, ctx_len):
    conv_k, cc = conv_w.shape[1], conv_w.shape[2]
    tr = ROW_TILE
    hb = tr // HALO
    n_halo_blocks = u.shape[0] // HALO
    kern = functools.partial(_conv_kernel, n_lat_tiles=n_lat_rows // tr, lat_tps=seq // tr,
                             ctx_tps=max(ctx_len // tr, 1), conv_k=conv_k)
    prev = lambda i: jnp.maximum(i * hb - 1, 0)
    nxt = lambda i: jnp.minimum((i + 1) * hb, n_halo_blocks - 1)
    return pl.pallas_call(
        kern,
        grid=(rows // tr,),
        in_specs=[
            pl.BlockSpec((tr, cc), lambda i: (i, 0)),
            pl.BlockSpec((tr, cc), lambda i: (i, 1)),
            pl.BlockSpec((HALO, cc), lambda i: (prev(i), 0)),
            pl.BlockSpec((HALO, cc), lambda i: (prev(i), 1)),
            pl.BlockSpec((HALO, cc), lambda i: (nxt(i), 0)),
            pl.BlockSpec((HALO, cc), lambda i: (nxt(i), 1)),
            pl.BlockSpec((None, conv_k, cc), lambda i: (layer, 0, 0)),
            pl.BlockSpec((None, 1, cc), lambda i: (layer, 0, 0)),
            pl.BlockSpec((None, 1, cc), lambda i: (layer, 0, 0)),
            pl.BlockSpec((None, 1, cc), lambda i: (layer, 0, 0)),
        ],
        out_specs=pl.BlockSpec((tr, cc), lambda i: (i, 0)),
        out_shape=jax.ShapeDtypeStruct((rows, cc), BF16),
        scratch_shapes=[
            pltpu.VMEM((tr + 2 * HALO, cc), F32),
            pltpu.VMEM((tr, cc), F32),
            pltpu.VMEM((V7X_SUBLANES - 1, tr + 2 * HALO - V7X_SUBLANES, V7X_LANES), F32),
        ],
        compiler_params=_params("parallel"),
        name="conv_branch",
    )(u, u, u, u, u, u, conv_w, conv_b3, ln_g3, ln_b3)


def _rope(x, cos, sin):
    half = V7X_LANES
    parts = []
    for p in range(x.shape[1] // half):
        xp = x[:, p * half:(p + 1) * half]
        parts.append(pltpu.roll(xp, half // 2, 1))
    return x * cos + jnp.concatenate(parts, axis=1) * sin


def _decay_terms(lgf, lgb, c):
    ia = lax.broadcasted_iota(jnp.int32, (c, c), 0)
    ib = lax.broadcasted_iota(jnp.int32, (c, c), 1)
    rel = (ia - ib).astype(F32)
    dm = jnp.where(rel >= 0, jnp.exp(lgf * jnp.maximum(rel, 0.0)), 0.0)
    dm = dm + jnp.where(rel <= 0, jnp.exp(lgb * jnp.maximum(-rel, 0.0)), 0.0)
    pos = lax.broadcasted_iota(jnp.int32, (c, 1), 0).astype(F32)
    return dm, pos


def _ret_kernel(lgf_ref, lgb_ref, q_ref, k_ref, v_ref, g_ref, cos_ref, sin_ref, s0f_ref, s0b_ref,
                y_ref, stf_ref, stb_ref, o_scr, q_scr, k_scr, sf_ref, sb_ref, *, layer, n_heads, rope, zero_init):
    h = pl.program_id(1)
    t, dh = q_ref.shape
    c = ROW_TILE
    n = t // c
    lgf = lgf_ref[layer * n_heads + h]
    lgb = lgb_ref[layer * n_heads + h]
    dm, pos = _decay_terms(lgf, lgb, c)
    xi_f = jnp.exp(lgf * (pos + 1.0))
    zeta_f = jnp.exp(lgf * (c - 1.0 - pos))
    xi_b = jnp.exp(lgb * (c - pos))
    zeta_b = jnp.exp(lgb * pos)
    full = jnp.full((1, 1), float(c), F32)
    cd_f = jnp.exp(lgf * full)
    cd_b = jnp.exp(lgb * full)
    scale = dh ** -0.5
    if zero_init:
        sf_ref[...] = jnp.zeros_like(sf_ref)
        sb_ref[...] = jnp.zeros_like(sb_ref)
    else:
        sf_ref[...] = s0f_ref[...]
        sb_ref[...] = s0b_ref[...]

    def fwd(i, carry):
        r = pl.ds(pl.multiple_of(i * c, c), c)
        q = q_ref[r, :]
        k = k_ref[r, :]
        if rope:
            q = _rope(q, cos_ref[r, :], sin_ref[r, :])
            k = _rope(k, cos_ref[r, :], sin_ref[r, :])
        qb = (q * scale).astype(BF16)
        vb = v_ref[r, :].astype(BF16)
        q_scr[r, :] = qb
        k_scr[r, :] = k
        s = _dot_nt(qb, k.astype(BF16)) * dm
        o = _dot(s.astype(BF16), vb) + xi_f * _dot(qb, sf_ref[...].astype(BF16))
        o_scr[r, :] = o
        sf_ref[...] = sf_ref[...] * cd_f + _dot_tn((k * zeta_f).astype(BF16), vb)
        return carry

    lax.fori_loop(0, n, fwd, 0)

    def bwd(j, carry):
        i = n - 1 - j
        r = pl.ds(pl.multiple_of(i * c, c), c)
        qb = q_scr[r, :]
        vb = v_ref[r, :].astype(BF16)
        o = o_scr[r, :] + xi_b * _dot(qb, sb_ref[...].astype(BF16))
        sb_ref[...] = sb_ref[...] * cd_b + _dot_tn((k_scr[r, :] * zeta_b).astype(BF16), vb)
        o = o * lax.rsqrt(jnp.mean(o * o, axis=-1, keepdims=True) + NORM_EPS)
        y_ref[r, :] = (o * _silu(g_ref[r, :])).astype(y_ref.dtype)
        return carry

    lax.fori_loop(0, n, bwd, 0)
    stf_ref[...] = sf_ref[...]
    stb_ref[...] = sb_ref[...]


def retention(lgf, lgb, u, cos, sin, s0f, s0b, layer, n_batch, t, row_off, q_off, n_heads, dh, rope, zero_init):
    rw = n_heads * dh
    ro = row_off // t
    qo, ko, vo, go = ((q_off + m * rw) // dh for m in range(4))
    kern = functools.partial(_ret_kernel, layer=layer, n_heads=n_heads, rope=rope, zero_init=zero_init)
    col = lambda off: pl.BlockSpec((t, dh), lambda b, h, *_: (b + ro, off + h))
    tab = pl.BlockSpec((t, dh), lambda b, h, *_: (0, 0))
    st = pl.BlockSpec((None, None, dh, dh), lambda b, h, *_: (b, h, 0, 0))
    grid_spec = pltpu.PrefetchScalarGridSpec(
        num_scalar_prefetch=2,
        grid=(n_batch, n_heads),
        in_specs=[col(qo), col(ko), col(vo), col(go), tab, tab, st, st],
        out_specs=[pl.BlockSpec((t, dh), lambda b, h, *_: (b, h)), st, st],
        scratch_shapes=[
            pltpu.VMEM((t, dh), F32), pltpu.VMEM((t, dh), BF16), pltpu.VMEM((t, dh), F32),
            pltpu.VMEM((dh, dh), F32), pltpu.VMEM((dh, dh), F32),
        ],
    )
    st_shape = jax.ShapeDtypeStruct((n_batch, n_heads, dh, dh), F32)
    return pl.pallas_call(
        kern,
        grid_spec=grid_spec,
        out_shape=[jax.ShapeDtypeStruct((n_batch * t, rw), BF16), st_shape, st_shape],
        compiler_params=_params("parallel", "parallel"),
        name="retention",
    )(lgf, lgb, u, u, u, u, cos, sin, s0f, s0b)


def _ctx_state_kernel(lgf_ref, lgb_ref, k_ref, v_ref, stf_ref, stb_ref, *, layer, n_heads):
    h = pl.program_id(1)
    t = k_ref.shape[0]
    lgf = lgf_ref[layer * n_heads + h]
    lgb = lgb_ref[layer * n_heads + h]
    pos = lax.broadcasted_iota(jnp.int32, (t, 1), 0).astype(F32)
    k = k_ref[...]
    vb = v_ref[...].astype(BF16)
    stf_ref[...] = _dot_tn((k * jnp.exp(lgf * (t - 1.0 - pos))).astype(BF16), vb)
    stb_ref[...] = _dot_tn((k * jnp.exp(lgb * pos)).astype(BF16), vb)


def ctx_states(lgf, lgb, kv, layer, n_batch, t, n_heads, dh):
    kern = functools.partial(_ctx_state_kernel, layer=layer, n_heads=n_heads)
    st = pl.BlockSpec((None, None, dh, dh), lambda b, h, *_: (b, h, 0, 0))
    grid_spec = pltpu.PrefetchScalarGridSpec(
        num_scalar_prefetch=2,
        grid=(n_batch, n_heads),
        in_specs=[
            pl.BlockSpec((t, dh), lambda b, h, *_: (b, h)),
            pl.BlockSpec((t, dh), lambda b, h, *_: (b, n_heads + h)),
        ],
        out_specs=[st, st],
    )
    st_shape = jax.ShapeDtypeStruct((n_batch, n_heads, dh, dh), F32)
    return pl.pallas_call(
        kern,
        grid_spec=grid_spec,
        out_shape=[st_shape, st_shape],
        compiler_params=_params("parallel", "parallel"),
        name="ctx_states",
    )(lgf, lgb, kv, kv)


def _row_copy(src_ref, src_row, dst_ref, dst_row, sem):
    return pltpu.make_async_copy(src_ref.at[pl.ds(src_row, 1), :], dst_ref.at[pl.ds(dst_row, 1), :], sem)


def _rows_done(like_ref, n_rows, sem):
    span = like_ref.at[pl.ds(0, n_rows), :]
    return pltpu.make_async_copy(span, span, sem)


def _dispatch_kernel(slot_ref, ztile_ref, h_ref, xs_ref, zero_ref, sem_ref, zsem_ref, *, n_rows, n_exp, bmx):
    i = pl.program_id(0)
    tt = h_ref.shape[0]

    n_tiles = xs_ref.shape[0] // bmx

    def zero_copy(tile):
        start = pl.multiple_of(tile * bmx, bmx)
        return pltpu.make_async_copy(zero_ref, xs_ref.at[pl.ds(start, bmx), :], zsem_ref)

    @pl.when(i == 0)
    def _():
        zero_ref[...] = jnp.zeros_like(zero_ref)
        n_used = ztile_ref[n_exp]

        def start(e, carry):
            @pl.when(ztile_ref[e] >= 0)
            def _():
                zero_copy(ztile_ref[e]).start()
            return carry

        def wait(e, carry):
            @pl.when(ztile_ref[e] >= 0)
            def _():
                zero_copy(ztile_ref[e]).wait()
            return carry

        def start_tail(j, carry):
            zero_copy(j).start()
            return carry

        def wait_tail(j, carry):
            zero_copy(j).wait()
            return carry

        lax.fori_loop(0, n_exp, start, 0)
        lax.fori_loop(n_used, n_tiles, start_tail, 0)
        lax.fori_loop(0, n_exp, wait, 0)
        lax.fori_loop(n_used, n_tiles, wait_tail, 0)

    def issue(g, carry):
        first = pl.multiple_of(g * V7X_SUBLANES, V7X_SUBLANES)
        for j in range(V7X_SUBLANES):
            for k in range(TOP_K):
                _row_copy(h_ref, first + j, xs_ref, slot_ref[k * n_rows + i * tt + first + j], sem_ref).start()
        return carry

    lax.fori_loop(0, tt // V7X_SUBLANES, issue, 0)
    _rows_done(xs_ref, TOP_K * tt, sem_ref).wait()


def moe_dispatch(slots, ztile, h, n_slots, bmx):
    rows, d = h.shape
    n_exp = ztile.shape[0] - 1
    tt = ROW_TILE
    kern = functools.partial(_dispatch_kernel, n_rows=rows, n_exp=n_exp, bmx=bmx)
    grid_spec = pltpu.PrefetchScalarGridSpec(
        num_scalar_prefetch=2,
        grid=(rows // tt,),
        in_specs=[pl.BlockSpec((tt, d), lambda i, *_: (i, 0))],
        out_specs=pl.BlockSpec(memory_space=pl.ANY),
        scratch_shapes=[pltpu.VMEM((bmx, d), h.dtype), pltpu.SemaphoreType.DMA(()), pltpu.SemaphoreType.DMA(())],
    )
    return pl.pallas_call(
        kern,
        grid_spec=grid_spec,
        out_shape=jax.ShapeDtypeStruct((n_slots, d), h.dtype),
        compiler_params=_params("arbitrary"),
        name="moe_dispatch",
    )(slots, ztile, h)


def _experts_kernel(texp_ref, nused_ref, x_ref, w1_ref, w3_ref, w2_ref, o_ref, w1b_ref, w3b_ref, w2b_ref):
    i = pl.program_id(0)
    valid = i < nused_ref[0]
    fresh = (i == 0) | (valid & (texp_ref[i] != texp_ref[jnp.maximum(i - 1, 0)]))

    @pl.when(fresh)
    def _():
        w1b_ref[...] = w1_ref[...].astype(BF16)
        w3b_ref[...] = w3_ref[...].astype(BF16)
        w2b_ref[...] = w2_ref[...].astype(BF16)

    @pl.when(valid)
    def _():
        x = _unpack_bf16(x_ref[...])
        a = _silu(_dot(x, w1b_ref[...])) * _dot(x, w3b_ref[...])
        o_ref[...] = _pack_halves(_dot(a.astype(BF16), w2b_ref[...]))

    @pl.when(jnp.logical_not(valid))
    def _():
        o_ref[...] = jnp.zeros_like(o_ref)


def moe_experts(tile_expert, n_used, xs, w1, w3, w2, layer, bmx):
    n_slots, dp = xs.shape
    d, hid = w1.shape[2], w1.shape[3]
    tile = lambda i, texp, nused: jnp.minimum(i, nused[0] - 1)
    rows = pl.BlockSpec((bmx, dp), lambda i, texp, nused: (tile(i, texp, nused), 0))
    out_rows = pl.BlockSpec((bmx, dp), lambda i, texp, nused: (i, 0))
    up = pl.BlockSpec((None, None, d, hid), lambda i, texp, nused: (layer, texp[tile(i, texp, nused)], 0, 0))
    down = pl.BlockSpec((None, None, hid, d), lambda i, texp, nused: (layer, texp[tile(i, texp, nused)], 0, 0))
    grid_spec = pltpu.PrefetchScalarGridSpec(
        num_scalar_prefetch=2,
        grid=(n_slots // bmx,),
        in_specs=[rows, up, up, down],
        out_specs=out_rows,
        scratch_shapes=[pltpu.VMEM((d, hid), BF16), pltpu.VMEM((d, hid), BF16), pltpu.VMEM((hid, d), BF16)],
    )
    return pl.pallas_call(
        _experts_kernel,
        grid_spec=grid_spec,
        out_shape=jax.ShapeDtypeStruct((n_slots, dp), xs.dtype),
        compiler_params=_params("arbitrary"),
        name="moe_experts",
    )(tile_expert, n_used, xs, w1, w3, w2)


def _combine_kernel(slot_ref, w_ref, ys_ref, o_ref, buf_ref, sem_ref, *, n_rows):
    i = pl.program_id(0)
    n = pl.num_programs(0)
    tt = o_ref.shape[0]

    def issue(tile, buf):
        def body(g, carry):
            first = pl.multiple_of(g * V7X_SUBLANES, V7X_SUBLANES)
            for j in range(V7X_SUBLANES):
                for k in range(TOP_K):
                    _row_copy(ys_ref, slot_ref[k * n_rows + tile * tt + first + j], buf_ref.at[buf],
                              k * tt + first + j, sem_ref.at[buf]).start()
            return carry

        lax.fori_loop(0, tt // V7X_SUBLANES, body, 0)

    @pl.when(i == 0)
    def _():
        issue(0, 0)

    @pl.when(i + 1 < n)
    def _():
        issue(i + 1, (i + 1) % 2)

    buf = i % 2
    _rows_done(ys_ref, TOP_K * tt, sem_ref.at[buf]).wait()
    w = w_ref[...]
    half = ys_ref.shape[1]
    acc_lo = jnp.zeros((tt, half), F32)
    acc_hi = jnp.zeros((tt, half), F32)
    for k in range(TOP_K):
        lo, hi = _unpack_halves(buf_ref[buf, k * tt:(k + 1) * tt, :])
        acc_lo = acc_lo + lo * w[:, k:k + 1]
        acc_hi = acc_hi + hi * w[:, k:k + 1]
    o_ref[:, :half] = acc_lo
    o_ref[:, half:] = acc_hi


def moe_combine(slots, wts, ys, rows):
    dp = ys.shape[1]
    tt = COMBINE_TT
    kern = functools.partial(_combine_kernel, n_rows=rows)
    grid_spec = pltpu.PrefetchScalarGridSpec(
        num_scalar_prefetch=1,
        grid=(rows // tt,),
        in_specs=[pl.BlockSpec((tt, TOP_K), lambda i, *_: (i, 0)), pl.BlockSpec(memory_space=pl.ANY)],
        out_specs=pl.BlockSpec((tt, 2 * dp), lambda i, *_: (i, 0)),
        scratch_shapes=[pltpu.VMEM((2, TOP_K * tt, dp), ys.dtype), pltpu.SemaphoreType.DMA((2,))],
    )
    return pl.pallas_call(
        kern,
        grid_spec=grid_spec,
        out_shape=jax.ShapeDtypeStruct((rows, 2 * dp), F32),
        compiler_params=_params("arbitrary"),
        name="moe_combine",
    )(slots, wts, ys)


def routing_tables(eidx, pos, counts, bmx, n_tiles):
    n_exp = counts.shape[0]
    padded = (counts + bmx - 1) // bmx * bmx
    ends = jnp.cumsum(padded)
    starts = ends - padded
    experts = jnp.arange(n_exp, dtype=jnp.int32)[:, None, None]
    slots = (jnp.sum(jnp.where(eidx[None] == experts, starts[:, None, None], 0), axis=0) + pos).reshape(-1)
    tile_ends = ends // bmx
    n_used = tile_ends[-1:]
    tiles = jnp.arange(n_tiles, dtype=jnp.int32)
    tile_expert = jnp.sum((tile_ends[None, :] <= tiles[:, None]).astype(jnp.int32), axis=1)
    tile_expert = jnp.minimum(tile_expert, n_exp - 1).astype(jnp.int32)
    ztile = jnp.concatenate([jnp.where(counts > 0, tile_ends - 1, -1), n_used]).astype(jnp.int32)
    return slots.astype(jnp.int32), ztile, tile_expert, n_used.astype(jnp.int32)


def _fused_experts_kernel(texp_ref, nused_ref, tab_ref, h_ref, w1_ref, w3_ref, w2_ref, ys_ref,
                          w1b_ref, w3b_ref, w2b_ref, xbuf0, xbuf1, ybuf0, ybuf1, gsem, ssem, zsem,
                          *, n_pairs, bmx):
    i = pl.program_id(0)
    n_steps = pl.num_programs(0)
    n_used = nused_ref[0]
    valid = i < n_used
    xbufs, ybufs = (xbuf0, xbuf1), (ybuf0, ybuf1)

    def row_groups(tile, copy_row):
        def body(g, carry):
            first = pl.multiple_of(g * COPY_GROUP, COPY_GROUP)
            for j in range(COPY_GROUP):
                copy_row(first + j, tab_ref[(tile + 1) * bmx + first + j])
            return carry

        lax.fori_loop(0, bmx // COPY_GROUP, body, 0)

    def gather(tile, buf):
        def copy_row(r, entry):
            src = h_ref.at[pl.ds(entry >> ROW_BITS, 1), :]
            pltpu.make_async_copy(src, xbufs[buf].at[pl.ds(r, 1), :], gsem.at[buf]).start()

        row_groups(tile, copy_row)

    def scatter(tile, buf):
        def copy_row(r, entry):
            dst = ys_ref.at[pl.ds(entry & ((1 << ROW_BITS) - 1), 1), :]
            pltpu.make_async_copy(ybufs[buf].at[pl.ds(r, 1), :], dst, ssem.at[buf]).start()

        row_groups(tile, copy_row)

    def tile_done(sem):
        return _rows_done(ys_ref, bmx, sem)

    @pl.when(i == 0)
    def _():
        ybuf1[...] = jnp.zeros_like(ybuf1)
        n_fill = (ys_ref.shape[0] - n_pairs) // bmx

        def fill(j):
            start = pl.multiple_of(n_pairs + j * bmx, bmx)
            return pltpu.make_async_copy(ybuf1, ys_ref.at[pl.ds(start, bmx), :], zsem)

        def fill_start(j, carry):
            fill(j).start()
            return carry

        def fill_wait(j, carry):
            fill(j).wait()
            return carry

        lax.fori_loop(0, n_fill, fill_start, 0)
        lax.fori_loop(0, n_fill, fill_wait, 0)
        gather(0, 0)

    fresh = (i == 0) | (valid & (texp_ref[i] != texp_ref[jnp.maximum(i - 1, 0)]))

    @pl.when(fresh)
    def _():
        w1b_ref[...] = w1_ref[...].astype(BF16)
        w3b_ref[...] = w3_ref[...].astype(BF16)
        w2b_ref[...] = w2_ref[...].astype(BF16)

    for p in range(2):
        q = 1 - p
        mine = i % 2 == p

        @pl.when(valid & mine)
        def _():
            tile_done(gsem.at[p]).wait()
            gather(i + 1, q)
            scatter(i - 1, q)
            x = _unpack_bf16(xbufs[p][...])
            a = _silu(_dot(x, w1b_ref[...])) * _dot(x, w3b_ref[...])
            ybufs[p][...] = _pack_halves(_dot(a.astype(BF16), w2b_ref[...]))
            tile_done(ssem.at[q]).wait()

        @pl.when((i == n_used) & mine)
        def _():
            tile_done(gsem.at[p]).wait()
            scatter(i - 1, q)
            tile_done(ssem.at[q]).wait()

        @pl.when((i == n_steps - 1) & valid & mine)
        def _():
            tile_done(gsem.at[q]).wait()
            scatter(i, p)
            tile_done(ssem.at[p]).wait()


def moe_experts_fused(tile_expert, n_used, table, h, w1, w3, w2, layer, n_tiles, bmx):
    n_tok, dp = h.shape
    d, hid = w1.shape[2], w1.shape[3]
    n_pairs = TOP_K * n_tok
    n_rows = (n_tiles + 1) * bmx
    assert n_rows <= 1 << ROW_BITS and n_tok <= 1 << (31 - ROW_BITS)
    expert = lambda i, texp, nused, *_: (layer, texp[jnp.minimum(i, nused[0] - 1)], 0, 0)
    up = pl.BlockSpec((None, None, d, hid), expert)
    down = pl.BlockSpec((None, None, hid, d), expert)
    kern = functools.partial(_fused_experts_kernel, n_pairs=n_pairs, bmx=bmx)
    grid_spec = pltpu.PrefetchScalarGridSpec(
        num_scalar_prefetch=3,
        grid=(n_tiles,),
        in_specs=[pl.BlockSpec(memory_space=pl.ANY), up, up, down],
        out_specs=pl.BlockSpec(memory_space=pl.ANY),
        scratch_shapes=[
            pltpu.VMEM((d, hid), BF16), pltpu.VMEM((d, hid), BF16), pltpu.VMEM((hid, d), BF16),
            pltpu.VMEM((bmx, dp), h.dtype), pltpu.VMEM((bmx, dp), h.dtype),
            pltpu.VMEM((bmx, dp), h.dtype), pltpu.VMEM((bmx, dp), h.dtype),
            pltpu.SemaphoreType.DMA((2,)), pltpu.SemaphoreType.DMA((2,)), pltpu.SemaphoreType.DMA(()),
        ],
    )
    return pl.pallas_call(
        kern,
        grid_spec=grid_spec,
        out_shape=jax.ShapeDtypeStruct((n_rows, dp), h.dtype),
        compiler_params=_params("arbitrary"),
        name="moe_experts_fused",
    )(tile_expert, n_used, table, h, w1, w3, w2)


def _dense_combine_kernel(w_ref, *refs):
    y_refs, o_ref = refs[:TOP_K], refs[TOP_K]
    tt = o_ref.shape[0]
    half = y_refs[0].shape[1]
    w = w_ref[...]
    rc, lc = 32, 2 * V7X_LANES
    for r0 in range(0, tt, rc):
        wr = w[r0:r0 + rc]
        for c0 in range(0, half, lc):
            acc_lo = jnp.zeros((rc, lc), F32)
            acc_hi = jnp.zeros((rc, lc), F32)
            for k in range(TOP_K):
                lo, hi = _unpack_halves(y_refs[k][r0:r0 + rc, c0:c0 + lc])
                acc_lo = acc_lo + lo * wr[:, k:k + 1]
                acc_hi = acc_hi + hi * wr[:, k:k + 1]
            o_ref[r0:r0 + rc, c0:c0 + lc] = acc_lo
            o_ref[r0:r0 + rc, half + c0:half + c0 + lc] = acc_hi


def moe_dense_combine(wts, ys, rows):
    dp = ys.shape[1]
    tt = COMBINE_TT
    per_k = rows // tt
    y_specs = [pl.BlockSpec((tt, dp), functools.partial(lambda i, k: (k * per_k + i, 0), k=k)) for k in range(TOP_K)]
    return pl.pallas_call(
        _dense_combine_kernel,
        grid=(rows // tt,),
        in_specs=[pl.BlockSpec((tt, TOP_K), lambda i: (i, 0))] + y_specs,
        out_specs=pl.BlockSpec((tt, 2 * dp), lambda i: (i, 0)),
        out_shape=jax.ShapeDtypeStruct((rows, 2 * dp), F32),
        compiler_params=_params("parallel"),
        name="moe_dense_combine",
    )(wts, *([ys] * TOP_K))


def fused_routing_tables(eidx, pos, counts, bmx, n_tiles):
    n_exp = counts.shape[0]
    n_tok = eidx.shape[1]
    n_pairs = TOP_K * n_tok
    n_slots = n_tiles * bmx
    n_pad = n_slots - n_pairs
    padded = (counts + bmx - 1) // bmx * bmx
    ends = jnp.cumsum(padded)
    starts = ends - padded
    experts = jnp.arange(n_exp, dtype=jnp.int32)
    slots = jnp.sum(jnp.where(eidx[None] == experts[:, None, None], starts[:, None, None], 0), axis=0) + pos
    pad_counts = jnp.concatenate([padded - counts, n_slots - ends[-1:]])
    pad_first = jnp.concatenate([starts + counts, ends[-1:]])
    pad_ends = jnp.cumsum(pad_counts)
    j = jnp.arange(n_pad, dtype=jnp.int32)
    owner = jnp.sum((pad_ends[None, :] <= j[:, None]).astype(jnp.int32), axis=1)
    owners = jnp.arange(n_exp + 1, dtype=jnp.int32)
    base = jnp.sum(jnp.where(owner[:, None] == owners[None, :], (pad_first - pad_ends + pad_counts)[None, :], 0), axis=1)
    pad_slot = base + j
    keys = jnp.concatenate([slots.reshape(-1), pad_slot]).astype(jnp.int32)
    rows = jnp.arange(n_slots, dtype=jnp.int32)
    toks = jnp.concatenate([jnp.tile(jnp.arange(n_tok, dtype=jnp.int32), TOP_K), jnp.zeros((n_pad,), jnp.int32)])
    _, rows, toks = lax.sort((keys, rows, toks), num_keys=1)
    spare = n_slots + jnp.arange(bmx, dtype=jnp.int32)
    rows = jnp.concatenate([spare, rows, spare])
    toks = jnp.concatenate([jnp.zeros_like(spare), toks, jnp.zeros_like(spare)])
    table = (toks << ROW_BITS) | rows
    tile_ends = ends // bmx
    n_used = tile_ends[-1:].astype(jnp.int32)
    tiles = jnp.arange(n_tiles, dtype=jnp.int32)
    tile_expert = jnp.sum((tile_ends[None, :] <= tiles[:, None]).astype(jnp.int32), axis=1)
    tile_expert = jnp.minimum(tile_expert, n_exp - 1).astype(jnp.int32)
    return table, tile_expert, n_used


def _glu_mm_kernel(h_ref, w1_ref, w3_ref, o_ref, w1b_ref, w3b_ref):
    @pl.when(pl.program_id(1) == 0)
    def _():
        w1b_ref[...] = w1_ref[...].astype(BF16)
        w3b_ref[...] = w3_ref[...].astype(BF16)

    h = _unpack_bf16(h_ref[...])
    o_ref[...] = (_silu(_dot(h, w1b_ref[...])) * _dot(h, w3b_ref[...])).astype(o_ref.dtype)


def glu_matmul(h, w1, w3, layer, rows, bm):
    k = w1.shape[1]
    n = w1.shape[2]
    bn = HID_CHUNK
    return pl.pallas_call(
        _glu_mm_kernel,
        grid=(n // bn, rows // bm),
        in_specs=[
            pl.BlockSpec((bm, h.shape[1]), lambda j, i: (i, 0)),
            pl.BlockSpec((None, k, bn), lambda j, i: (layer, 0, j)),
            pl.BlockSpec((None, k, bn), lambda j, i: (layer, 0, j)),
        ],
        out_specs=pl.BlockSpec((bm, bn), lambda j, i: (i, j)),
        out_shape=jax.ShapeDtypeStruct((rows, n), BF16),
        scratch_shapes=[pltpu.VMEM((k, bn), BF16), pltpu.VMEM((k, bn), BF16)],
        compiler_params=_params("arbitrary", "arbitrary"),
        name="glu_matmul",
    )(h, w1, w3)


def _rope_tables(seq, dh):
    rows = seq // GRID_W
    row = jnp.repeat(jnp.arange(rows, dtype=F32), GRID_W)
    col = jnp.tile(jnp.arange(GRID_W, dtype=F32), rows)
    quarter = dh // 4
    inv_freq = ROPE_BASE ** (-jnp.arange(quarter, dtype=F32) / quarter)
    ang_r = row[:, None] * inv_freq[None, :]
    ang_c = col[:, None] * inv_freq[None, :]
    cos = jnp.concatenate([jnp.cos(ang_r), jnp.cos(ang_r), jnp.cos(ang_c), jnp.cos(ang_c)], axis=-1)
    sin = jnp.concatenate([-jnp.sin(ang_r), jnp.sin(ang_r), -jnp.sin(ang_c), jnp.sin(ang_c)], axis=-1)
    return cos, sin


def kernel(x, c, ctx, c_ctx, mod_w, mod_b, norm1_g, norm2_g, w_in, conv_w, conv_b, conv_ln_g, conv_ln_b,
           w_conv_out, ret_log_gamma_fwd, ret_log_gamma_bwd, w_ret_out, w_merge_out, router_w, router_bias,
           exp_w1, exp_w3, exp_w2, shared_w1, shared_w3, shared_w2, final_g):
    b, s, d = x.shape
    n_ctx = ctx.shape[1]
    depth = mod_w.shape[0]
    cc = conv_w.shape[2]
    n_heads = ret_log_gamma_fwd.shape[1]
    rw = w_ret_out.shape[1]
    dh = rw // n_heads
    n_exp = router_w.shape[2]
    n_lat, n_cx = b * s, b * n_ctx
    n_all = n_lat + n_cx
    bm = min(MM_BM, s, n_cx)
    assert b + 1 <= 8 and s % ROW_TILE == 0 and n_ctx % ROW_TILE == 0 and dh == 2 * V7X_LANES
    assert s % bm == 0 and n_cx % bm == 0 and s % MOE_TM == 0 and n_cx % MOE_TM == 0
    q_off, gate_off = 2 * cc, 2 * cc + 4 * rw

    xa = jnp.concatenate([x.reshape(n_lat, d), ctx.reshape(n_cx, d)], axis=0)
    c8 = jnp.zeros((8, d), F32).at[:b].set(c).at[b].set(c_ctx)
    mod_b3 = mod_b.reshape(depth, 1, -1)
    n1g, n2g = norm1_g.reshape(depth, 1, d), norm2_g.reshape(depth, 1, d)
    conv_b3, ln_g3, ln_b3 = (a.reshape(depth, 1, cc) for a in (conv_b, conv_ln_g, conv_ln_b))
    lgf, lgb = ret_log_gamma_fwd.reshape(-1), ret_log_gamma_bwd.reshape(-1)
    rw_t = jnp.swapaxes(router_w, 1, 2)
    rw_hi = rw_t.astype(BF16)
    rw_lo = (rw_t - rw_hi.astype(F32)).astype(BF16)
    rw_cat = jnp.concatenate([rw_hi, rw_lo], axis=1)
    bias3 = router_bias.reshape(depth, n_exp, 1)
    cos, sin = _rope_tables(s, dh)
    zero_st = jnp.zeros((b, n_heads, dh, dh), F32)

    for layer in range(depth):
        last = layer == depth - 1
        rows = n_lat if last else n_all
        seq_args = dict(n_lat_rows=n_lat, seq=s, n_batch=b)
        mod3 = modulation(c8, mod_w, mod_b3, layer).reshape(8 * N_MOD, 1, d)
        h1 = norm_mod(xa, n1g, mod3, layer, 0, 1, n_all, **seq_args)
        if last:
            u = matmul(h1, w_in, layer, n_lat, 0, 0, w_in.shape[2], bm)
            kv_c = matmul(h1, w_in, layer, n_cx, n_lat, q_off + rw, 2 * rw, bm)
            st_f, st_b = ctx_states(lgf, lgb, kv_c, layer, b, n_ctx, n_heads, dh)
        else:
            u = matmul(h1, w_in, layer, n_all, 0, 0, w_in.shape[2], bm)
            yr_c, st_f, st_b = retention(lgf, lgb, u, cos, sin, zero_st, zero_st, layer, b, n_ctx, n_lat, q_off,
                                         n_heads, dh, rope=False, zero_init=True)
        yc = conv_branch(u, conv_w, conv_b3, ln_g3, ln_b3, layer, rows, n_lat, s, n_ctx)
        yr, _, _ = retention(lgf, lgb, u, cos, sin, st_f, st_b, layer, b, s, 0, q_off, n_heads, dh,
                             rope=True, zero_init=False)
        if not last:
            yr = jnp.concatenate([yr, yr_c], axis=0)
        mixed = merge_branches(yc, yr, u, w_conv_out, w_ret_out, layer, rows, gate_off, bm)
        xa = resid_matmul(mixed, w_merge_out, xa, mod3, layer, 2, rows, bm=bm, **seq_args)
        h2, eidx, pos, wts, cnt = norm_router(xa, n2g, mod3, rw_cat, rw_hi, bias3, layer, 3, 4, rows, **seq_args)
        n_tiles = TOP_K * rows // MOE_BMX + n_exp
        table, tile_expert, n_used = fused_routing_tables(eidx, pos, cnt[:, 0].astype(jnp.int32), MOE_BMX, n_tiles)
        ys = moe_experts_fused(tile_expert, n_used, table, h2, exp_w1, exp_w3, exp_w2, layer, n_tiles, MOE_BMX)
        routed = moe_dense_combine(wts.T, ys, rows)
        act = glu_matmul(h2, shared_w1, shared_w3, layer, rows, MOE_TM)
        xa = resid_matmul(act, shared_w2, xa, mod3, layer, 5, rows, bm=bm, extra=routed, **seq_args)
    out = final_norm(xa, final_g.reshape(1, d), n_lat)
    return out.reshape(b, s, d)
```

```python
import functools

import jax
import jax.numpy as jnp
from jax import lax
from jax.experimental import pallas as pl
from jax.experimental.pallas import tpu as pltpu

GRID_W = 64
N_GROUPS = 8
TOPK_GROUPS = 4
TOP_K = 8
ROUTED_SCALE = 2.5
N_MOD = 6
NORM_EPS = 1e-6
ROPE_BASE = 10000.0

V7X_LANES = 128
V7X_SUBLANES = 8
V7X_VMEM_BYTES = 64 * 1024 * 1024
VMEM_LIMIT = V7X_VMEM_BYTES - 8 * 1024 * 1024

ROW_TILE = 256
HALO = 16
MM_BM = 1024
MM_BN = 512
MOE_TM = 512
MOE_BMX = 256
COMBINE_TT = 128
COPY_GROUP = 32
ROW_BITS = 17
HID_CHUNK = 256

BF16 = jnp.bfloat16
F32 = jnp.float32


def _params(*sem):
    return pltpu.CompilerParams(dimension_semantics=sem, vmem_limit_bytes=VMEM_LIMIT)


def _dot(a, b):
    return jnp.dot(a, b, preferred_element_type=F32)


def _dot_nt(a, b):
    return lax.dot_general(a, b, (((1,), (1,)), ((), ())), preferred_element_type=F32)


def _dot_tn(a, b):
    return lax.dot_general(a, b, (((0,), (0,)), ((), ())), preferred_element_type=F32)


def _silu(x):
    return x * jax.nn.sigmoid(x)


def _pack_halves(v):
    half = v.shape[1] // 2
    lo = pltpu.bitcast(v[:, :half].astype(BF16).astype(F32), jnp.uint32)
    hi = pltpu.bitcast(v[:, half:].astype(BF16).astype(F32), jnp.uint32)
    return (hi & jnp.uint32(0xFFFF0000)) | (lo >> 16)


def _unpack_halves(p):
    lo = pltpu.bitcast(p << 16, F32)
    hi = pltpu.bitcast(p & jnp.uint32(0xFFFF0000), F32)
    return lo, hi


def _unpack_bf16(p):
    lo, hi = _unpack_halves(p)
    return jnp.concatenate([lo.astype(BF16), hi.astype(BF16)], axis=1)


def _mod_row(i, n_lat_tiles, tiles_per_seq, n_batch):
    return jnp.where(i < n_lat_tiles, i // tiles_per_seq, n_batch)


def _mod_kernel(c_ref, w_ref, b_ref, o_ref):
    c = c_ref[...]
    sc = _silu(c).astype(BF16)
    o_ref[...] = _dot(sc, w_ref[...].astype(BF16)) + b_ref[...]


def modulation(c8, mod_w, mod_b3, layer):
    d = c8.shape[1]
    n = mod_w.shape[2]
    bn = MM_BN
    return pl.pallas_call(
        _mod_kernel,
        grid=(n // bn,),
        in_specs=[
            pl.BlockSpec((8, d), lambda j: (0, 0)),
            pl.BlockSpec((None, d, bn), lambda j: (layer, 0, j)),
            pl.BlockSpec((None, 1, bn), lambda j: (layer, 0, j)),
        ],
        out_specs=pl.BlockSpec((8, bn), lambda j: (0, j)),
        out_shape=jax.ShapeDtypeStruct((8, n), F32),
        compiler_params=_params("arbitrary"),
        name="modulation",
    )(c8, mod_w, mod_b3)


def _rms(x, g):
    return x * lax.rsqrt(jnp.mean(x * x, axis=-1, keepdims=True) + NORM_EPS) * g


def _norm_mod_kernel(x_ref, g_ref, sh_ref, sc_ref, o_ref):
    h = _rms(x_ref[...], g_ref[...]) * (1.0 + sc_ref[...]) + sh_ref[...]
    o_ref[...] = h.astype(o_ref.dtype)


def norm_mod(x, g3, mod3, layer, sh_idx, sc_idx, rows, n_lat_rows, seq, n_batch):
    d = x.shape[1]
    tr = ROW_TILE
    row = functools.partial(_mod_row, n_lat_tiles=n_lat_rows // tr, tiles_per_seq=seq // tr, n_batch=n_batch)
    return pl.pallas_call(
        _norm_mod_kernel,
        grid=(rows // tr,),
        in_specs=[
            pl.BlockSpec((tr, d), lambda i: (i, 0)),
            pl.BlockSpec((None, 1, d), lambda i: (layer, 0, 0)),
            pl.BlockSpec((None, 1, d), lambda i: (row(i) * N_MOD + sh_idx, 0, 0)),
            pl.BlockSpec((None, 1, d), lambda i: (row(i) * N_MOD + sc_idx, 0, 0)),
        ],
        out_specs=pl.BlockSpec((tr, d), lambda i: (i, 0)),
        out_shape=jax.ShapeDtypeStruct((rows, d), BF16),
        compiler_params=_params("parallel"),
        name="norm_mod",
    )(x, g3, mod3, mod3)


def _final_norm_kernel(x_ref, g_ref, o_ref):
    o_ref[...] = _rms(x_ref[...], g_ref[...])


def final_norm(x, g2, rows):
    d = x.shape[1]
    tr = ROW_TILE
    return pl.pallas_call(
        _final_norm_kernel,
        grid=(rows // tr,),
        in_specs=[pl.BlockSpec((tr, d), lambda i: (i, 0)), pl.BlockSpec((1, d), lambda i: (0, 0))],
        out_specs=pl.BlockSpec((tr, d), lambda i: (i, 0)),
        out_shape=jax.ShapeDtypeStruct((rows, d), F32),
        compiler_params=_params("parallel"),
        name="final_norm",
    )(x, g2)


def _route(logits_t, bias):
    n_exp, tt = logits_t.shape
    per = n_exp // N_GROUPS
    s = jax.nn.sigmoid(logits_t)
    biased = s + bias
    g3 = biased.reshape(N_GROUPS, per, tt)
    mem = lax.broadcasted_iota(jnp.int32, g3.shape, 1)
    m1 = jnp.max(g3, axis=1, keepdims=True)
    first = jnp.min(jnp.where(g3 == m1, mem, per), axis=1, keepdims=True)
    m2 = jnp.max(jnp.where(mem == first, -jnp.inf, g3), axis=1, keepdims=True)
    gs = m1 + m2
    gid = lax.broadcasted_iota(jnp.int32, gs.shape, 0)
    grank = jnp.zeros(gs.shape, jnp.int32)
    for j in range(N_GROUPS):
        other = gs[j:j + 1]
        ahead = (other > gs) | ((other == gs) & (gid > j))
        grank = grank + ahead.astype(jnp.int32)
    gmask = grank < TOPK_GROUPS
    masked = jnp.where(gmask, g3, -jnp.inf).reshape(n_exp, tt)
    eid = lax.broadcasted_iota(jnp.int32, masked.shape, 0)
    rank = jnp.zeros(masked.shape, jnp.int32)
    for j in range(n_exp):
        other = masked[j:j + 1, :]
        ahead = (other > masked) | ((other == masked) & (eid > j))
        rank = rank + ahead.astype(jnp.int32)
    sel = jnp.where(rank < TOP_K, s, 0.0)
    return sel / jnp.sum(sel, axis=0, keepdims=True) * ROUTED_SCALE, rank


def _norm_router_kernel(x_ref, g_ref, sh_ref, sc_ref, wcat_ref, whi_ref, bias_ref,
                        h_ref, eidx_ref, pos_ref, wts_ref, cnt_ref):
    i = pl.program_id(0)
    h = _rms(x_ref[...], g_ref[...]) * (1.0 + sc_ref[...]) + sh_ref[...]
    h_hi = h.astype(BF16)
    h_lo = (h - h_hi.astype(F32)).astype(BF16)
    h_ref[...] = _pack_halves(h)
    n_exp = whi_ref.shape[0]
    lt = _dot_nt(wcat_ref[...], h_hi)
    logits_t = lt[:n_exp] + lt[n_exp:] + _dot_nt(whi_ref[...], h_lo)
    comb, rank = _route(logits_t, bias_ref[...])
    tt = comb.shape[1]

    @pl.when(i == 0)
    def _():
        cnt_ref[...] = jnp.zeros_like(cnt_ref)

    chosen = (rank < TOP_K).astype(F32)
    ta = lax.broadcasted_iota(jnp.int32, (tt, tt), 0)
    tb = lax.broadcasted_iota(jnp.int32, (tt, tt), 1)
    before = (ta < tb).astype(BF16)
    prefix = _dot(chosen.astype(BF16), before)
    base = cnt_ref[:, 0:1]
    pos = base + prefix
    cnt_ref[...] = cnt_ref[...] + jnp.sum(chosen, axis=1, keepdims=True)
    eid = lax.broadcasted_iota(jnp.int32, comb.shape, 0).astype(F32)
    for k in range(TOP_K):
        mk = rank == k
        eidx_ref[k:k + 1, :] = jnp.sum(jnp.where(mk, eid, 0.0), axis=0, keepdims=True).astype(jnp.int32)
        pos_ref[k:k + 1, :] = jnp.sum(jnp.where(mk, pos, 0.0), axis=0, keepdims=True).astype(jnp.int32)
        wts_ref[k:k + 1, :] = jnp.sum(jnp.where(mk, comb, 0.0), axis=0, keepdims=True)


def norm_router(x, g3, mod3, wcat_t, whi_t, bias3, layer, sh_idx, sc_idx, rows, n_lat_rows, seq, n_batch):
    d = x.shape[1]
    n_exp = whi_t.shape[1]
    tr = ROW_TILE
    row = functools.partial(_mod_row, n_lat_tiles=n_lat_rows // tr, tiles_per_seq=seq // tr, n_batch=n_batch)
    lists = pl.BlockSpec((TOP_K, tr), lambda i: (0, i))
    return pl.pallas_call(
        _norm_router_kernel,
        grid=(rows // tr,),
        in_specs=[
            pl.BlockSpec((tr, d), lambda i: (i, 0)),
            pl.BlockSpec((None, 1, d), lambda i: (layer, 0, 0)),
            pl.BlockSpec((None, 1, d), lambda i: (row(i) * N_MOD + sh_idx, 0, 0)),
            pl.BlockSpec((None, 1, d), lambda i: (row(i) * N_MOD + sc_idx, 0, 0)),
            pl.BlockSpec((None, 2 * n_exp, d), lambda i: (layer, 0, 0)),
            pl.BlockSpec((None, n_exp, d), lambda i: (layer, 0, 0)),
            pl.BlockSpec((None, n_exp, 1), lambda i: (layer, 0, 0)),
        ],
        out_specs=[
            pl.BlockSpec((tr, d // 2), lambda i: (i, 0)),
            lists, lists, lists,
            pl.BlockSpec((n_exp, V7X_LANES), lambda i: (0, 0)),
        ],
        out_shape=[
            jax.ShapeDtypeStruct((rows, d // 2), jnp.uint32),
            jax.ShapeDtypeStruct((TOP_K, rows), jnp.int32),
            jax.ShapeDtypeStruct((TOP_K, rows), jnp.int32),
            jax.ShapeDtypeStruct((TOP_K, rows), F32),
            jax.ShapeDtypeStruct((n_exp, V7X_LANES), F32),
        ],
        compiler_params=_params("arbitrary"),
        name="norm_router",
    )(x, g3, mod3, mod3, wcat_t, whi_t, bias3)


def _mm_kernel(a_ref, w_ref, o_ref, wb_ref):
    @pl.when(pl.program_id(1) == 0)
    def _():
        wb_ref[...] = w_ref[...].astype(BF16)

    o_ref[...] = _dot(a_ref[...], wb_ref[...]).astype(o_ref.dtype)


def matmul(a, w, layer, rows, row_off, col_off, ncols, bm, out_dtype=F32):
    k = a.shape[1]
    bn = MM_BN
    ro, co = row_off // bm, col_off // bn
    return pl.pallas_call(
        _mm_kernel,
        grid=(ncols // bn, rows // bm),
        in_specs=[
            pl.BlockSpec((bm, k), lambda j, i: (i + ro, 0)),
            pl.BlockSpec((None, k, bn), lambda j, i: (layer, 0, j + co)),
        ],
        out_specs=pl.BlockSpec((bm, bn), lambda j, i: (i, j)),
        out_shape=jax.ShapeDtypeStruct((rows, ncols), out_dtype),
        scratch_shapes=[pltpu.VMEM((k, bn), BF16)],
        compiler_params=_params("arbitrary", "arbitrary"),
        name="matmul",
    )(a, w)


def _merge_kernel(yc_ref, yr_ref, gc_ref, gr_ref, wc_ref, wr_ref, o_ref, wcb_ref, wrb_ref):
    @pl.when(pl.program_id(1) == 0)
    def _():
        wcb_ref[...] = wc_ref[...].astype(BF16)
        wrb_ref[...] = wr_ref[...].astype(BF16)

    y = jax.nn.sigmoid(gc_ref[...]) * _dot(yc_ref[...], wcb_ref[...])
    y = y + jax.nn.sigmoid(gr_ref[...]) * _dot(yr_ref[...], wrb_ref[...])
    o_ref[...] = y.astype(o_ref.dtype)


def merge_branches(yc, yr, u, w_conv_out, w_ret_out, layer, rows, gate_off, bm):
    kc, kr = yc.shape[1], yr.shape[1]
    d = w_conv_out.shape[2]
    bn = MM_BN
    gco, gro = gate_off // bn, (gate_off + d) // bn
    return pl.pallas_call(
        _merge_kernel,
        grid=(d // bn, rows // bm),
        in_specs=[
            pl.BlockSpec((bm, kc), lambda j, i: (i, 0)),
            pl.BlockSpec((bm, kr), lambda j, i: (i, 0)),
            pl.BlockSpec((bm, bn), lambda j, i: (i, j + gco)),
            pl.BlockSpec((bm, bn), lambda j, i: (i, j + gro)),
            pl.BlockSpec((None, kc, bn), lambda j, i: (layer, 0, j)),
            pl.BlockSpec((None, kr, bn), lambda j, i: (layer, 0, j)),
        ],
        out_specs=pl.BlockSpec((bm, bn), lambda j, i: (i, j)),
        out_shape=jax.ShapeDtypeStruct((rows, d), BF16),
        scratch_shapes=[pltpu.VMEM((kc, bn), BF16), pltpu.VMEM((kr, bn), BF16)],
        compiler_params=_params("arbitrary", "arbitrary"),
        name="merge_branches",
    )(yc, yr, u, u, w_conv_out, w_ret_out)


def _resid_mm_kernel(a_ref, w_ref, x_ref, g_ref, o_ref, wb_ref):
    @pl.when(pl.program_id(1) == 0)
    def _():
        wb_ref[...] = w_ref[...].astype(BF16)

    o_ref[...] = x_ref[...] + g_ref[...] * _dot(a_ref[...], wb_ref[...])


def _resid_mm_add_kernel(a_ref, w_ref, x_ref, g_ref, e_ref, o_ref, wb_ref):
    @pl.when(pl.program_id(1) == 0)
    def _():
        wb_ref[...] = w_ref[...].astype(BF16)

    o_ref[...] = x_ref[...] + g_ref[...] * (e_ref[...] + _dot(a_ref[...], wb_ref[...]))


def resid_matmul(a, w, x, mod3, layer, g_idx, rows, n_lat_rows, seq, n_batch, bm, extra=None):
    k = a.shape[1]
    d = w.shape[2]
    bn = MM_BN
    row = functools.partial(_mod_row, n_lat_tiles=n_lat_rows // bm, tiles_per_seq=seq // bm, n_batch=n_batch)
    tile = pl.BlockSpec((bm, bn), lambda j, i: (i, j))
    in_specs = [
        pl.BlockSpec((bm, k), lambda j, i: (i, 0)),
        pl.BlockSpec((None, k, bn), lambda j, i: (layer, 0, j)),
        tile,
        pl.BlockSpec((None, 1, bn), lambda j, i: (row(i) * N_MOD + g_idx, 0, j)),
    ]
    operands = [a, w, x, mod3]
    if extra is not None:
        in_specs.append(tile)
        operands.append(extra)
    return pl.pallas_call(
        _resid_mm_kernel if extra is None else _resid_mm_add_kernel,
        grid=(d // bn, rows // bm),
        in_specs=in_specs,
        out_specs=tile,
        out_shape=jax.ShapeDtypeStruct((rows, d), F32),
        scratch_shapes=[pltpu.VMEM((k, bn), BF16)],
        compiler_params=_params("arbitrary", "arbitrary"),
        name="resid_matmul",
    )(*operands)


def _conv_kernel(a_ref, g_ref, ap_ref, gp_ref, an_ref, gn_ref, w_ref, b_ref, lng_ref, lnb_ref, o_ref,
                 buf_ref, acc_ref, shift_ref, *, n_lat_tiles, lat_tps, ctx_tps, conv_k):
    i = pl.program_id(0)
    tr, cc = a_ref.shape
    is_lat = i < n_lat_tiles
    pos = jnp.where(is_lat, i % lat_tps, (i - n_lat_tiles) % ctx_tps)
    tps = jnp.where(is_lat, lat_tps, ctx_tps)
    keep_prev = (pos != 0).astype(F32)
    keep_next = (pos != tps - 1).astype(F32)
    buf_ref[HALO:HALO + tr, :] = a_ref[...] * jax.nn.sigmoid(g_ref[...])
    buf_ref[0:HALO, :] = ap_ref[...] * jax.nn.sigmoid(gp_ref[...]) * keep_prev
    buf_ref[HALO + tr:, :] = an_ref[...] * jax.nn.sigmoid(gn_ref[...]) * keep_next
    pad = (conv_k - 1) // 2
    rc = 128
    n_rc = tr // rc
    span = shift_ref.shape[1]

    def chunk(t, carry):
        lanes = pl.ds(pl.multiple_of(t * V7X_LANES, V7X_LANES), V7X_LANES)
        for s in range(1, V7X_SUBLANES):
            shift_ref[s - 1] = buf_ref[s:s + span, lanes]
        for r in range(n_rc):
            acc = jnp.zeros((rc, V7X_LANES), F32)
            for kk in range(conv_k):
                whole, s = divmod(HALO - pad + kk, V7X_SUBLANES)
                start = r * rc + whole * V7X_SUBLANES
                if s == 0:
                    window = buf_ref[start:start + rc, lanes]
                else:
                    window = shift_ref[s - 1, start:start + rc, :]
                acc = acc + window * w_ref[kk:kk + 1, lanes]
            acc_ref[r * rc:(r + 1) * rc, lanes] = acc
        return carry

    lax.fori_loop(0, cc // V7X_LANES, chunk, 0)
    y = acc_ref[...] + b_ref[...]
    mu = jnp.mean(y, axis=-1, keepdims=True)
    yc = y - mu
    var = jnp.mean(yc * yc, axis=-1, keepdims=True)
    z = yc * lax.rsqrt(var + NORM_EPS) * lng_ref[...] + lnb_ref[...]
    o_ref[...] = _silu(z).astype(o_ref.dtype)


def conv_branch(u, conv_w, conv_b3, ln_g3, ln_b3, layer, rows, n_lat_rows, seq, ctx_len):
    conv_k, cc = conv_w.shape[1], conv_w.shape[2]
    tr = ROW_TILE
    hb = tr // HALO
    n_halo_blocks = u.shape[0] // HALO
    kern = functools.partial(_conv_kernel, n_lat_tiles=n_lat_rows // tr, lat_tps=seq // tr,
                             ctx_tps=max(ctx_len // tr, 1), conv_k=conv_k)
    prev = lambda i: jnp.maximum(i * hb - 1, 0)
    nxt = lambda i: jnp.minimum((i + 1) * hb, n_halo_blocks - 1)
    return pl.pallas_call(
        kern,
        grid=(rows // tr,),
        in_specs=[
            pl.BlockSpec((tr, cc), lambda i: (i, 0)),
            pl.BlockSpec((tr, cc), lambda i: (i, 1)),
            pl.BlockSpec((HALO, cc), lambda i: (prev(i), 0)),
            pl.BlockSpec((HALO, cc), lambda i: (prev(i), 1)),
            pl.BlockSpec((HALO, cc), lambda i: (nxt(i), 0)),
            pl.BlockSpec((HALO, cc), lambda i: (nxt(i), 1)),
            pl.BlockSpec((None, conv_k, cc), lambda i: (layer, 0, 0)),
            pl.BlockSpec((None, 1, cc), lambda i: (layer, 0, 0)),
            pl.BlockSpec((None, 1, cc), lambda i: (layer, 0, 0)),
            pl.BlockSpec((None, 1, cc), lambda i: (layer, 0, 0)),
        ],
        out_specs=pl.BlockSpec((tr, cc), lambda i: (i, 0)),
        out_shape=jax.ShapeDtypeStruct((rows, cc), BF16),
        scratch_shapes=[
            pltpu.VMEM((tr + 2 * HALO, cc), F32),
            pltpu.VMEM((tr, cc), F32),
            pltpu.VMEM((V7X_SUBLANES - 1, tr + 2 * HALO - V7X_SUBLANES, V7X_LANES), F32),
        ],
        compiler_params=_params("parallel"),
        name="conv_branch",
    )(u, u, u, u, u, u, conv_w, conv_b3, ln_g3, ln_b3)


def _rope(x, cos, sin):
    half = V7X_LANES
    parts = []
    for p in range(x.shape[1] // half):
        xp = x[:, p * half:(p + 1) * half]
        parts.append(pltpu.roll(xp, half // 2, 1))
    return x * cos + jnp.concatenate(parts, axis=1) * sin


def _decay_terms(lgf, lgb, c):
    ia = lax.broadcasted_iota(jnp.int32, (c, c), 0)
    ib = lax.broadcasted_iota(jnp.int32, (c, c), 1)
    rel = (ia - ib).astype(F32)
    dm = jnp.where(rel >= 0, jnp.exp(lgf * jnp.maximum(rel, 0.0)), 0.0)
    dm = dm + jnp.where(rel <= 0, jnp.exp(lgb * jnp.maximum(-rel, 0.0)), 0.0)
    pos = lax.broadcasted_iota(jnp.int32, (c, 1), 0).astype(F32)
    return dm, pos


def _ret_kernel(lgf_ref, lgb_ref, q_ref, k_ref, v_ref, g_ref, cos_ref, sin_ref, s0f_ref, s0b_ref,
                y_ref, stf_ref, stb_ref, o_scr, q_scr, k_scr, sf_ref, sb_ref, *, layer, n_heads, rope, zero_init):
    h = pl.program_id(1)
    t, dh = q_ref.shape
    c = ROW_TILE
    n = t // c
    lgf = lgf_ref[layer * n_heads + h]
    lgb = lgb_ref[layer * n_heads + h]
    dm, pos = _decay_terms(lgf, lgb, c)
    xi_f = jnp.exp(lgf * (pos + 1.0))
    zeta_f = jnp.exp(lgf * (c - 1.0 - pos))
    xi_b = jnp.exp(lgb * (c - pos))
    zeta_b = jnp.exp(lgb * pos)
    full = jnp.full((1, 1), float(c), F32)
    cd_f = jnp.exp(lgf * full)
    cd_b = jnp.exp(lgb * full)
    scale = dh ** -0.5
    if zero_init:
        sf_ref[...] = jnp.zeros_like(sf_ref)
        sb_ref[...] = jnp.zeros_like(sb_ref)
    else:
        sf_ref[...] = s0f_ref[...]
        sb_ref[...] = s0b_ref[...]

    def fwd(i, carry):
        r = pl.ds(pl.multiple_of(i * c, c), c)
        q = q_ref[r, :]
        k = k_ref[r, :]
        if rope:
            q = _rope(q, cos_ref[r, :], sin_ref[r, :])
            k = _rope(k, cos_ref[r, :], sin_ref[r, :])
        qb = (q * scale).astype(BF16)
        vb = v_ref[r, :].astype(BF16)
        q_scr[r, :] = qb
        k_scr[r, :] = k
        s = _dot_nt(qb, k.astype(BF16)) * dm
        o = _dot(s.astype(BF16), vb) + xi_f * _dot(qb, sf_ref[...].astype(BF16))
        o_scr[r, :] = o
        sf_ref[...] = sf_ref[...] * cd_f + _dot_tn((k * zeta_f).astype(BF16), vb)
        return carry

    lax.fori_loop(0, n, fwd, 0)

    def bwd(j, carry):
        i = n - 1 - j
        r = pl.ds(pl.multiple_of(i * c, c), c)
        qb = q_scr[r, :]
        vb = v_ref[r, :].astype(BF16)
        o = o_scr[r, :] + xi_b * _dot(qb, sb_ref[...].astype(BF16))
        sb_ref[...] = sb_ref[...] * cd_b + _dot_tn((k_scr[r, :] * zeta_b).astype(BF16), vb)
        o = o * lax.rsqrt(jnp.mean(o * o, axis=-1, keepdims=True) + NORM_EPS)
        y_ref[r, :] = (o * _silu(g_ref[r, :])).astype(y_ref.dtype)
        return carry

    lax.fori_loop(0, n, bwd, 0)
    stf_ref[...] = sf_ref[...]
    stb_ref[...] = sb_ref[...]


def retention(lgf, lgb, u, cos, sin, s0f, s0b, layer, n_batch, t, row_off, q_off, n_heads, dh, rope, zero_init):
    rw = n_heads * dh
    ro = row_off // t
    qo, ko, vo, go = ((q_off + m * rw) // dh for m in range(4))
    kern = functools.partial(_ret_kernel, layer=layer, n_heads=n_heads, rope=rope, zero_init=zero_init)
    col = lambda off: pl.BlockSpec((t, dh), lambda b, h, *_: (b + ro, off + h))
    tab = pl.BlockSpec((t, dh), lambda b, h, *_: (0, 0))
    st = pl.BlockSpec((None, None, dh, dh), lambda b, h, *_: (b, h, 0, 0))
    grid_spec = pltpu.PrefetchScalarGridSpec(
        num_scalar_prefetch=2,
        grid=(n_batch, n_heads),
        in_specs=[col(qo), col(ko), col(vo), col(go), tab, tab, st, st],
        out_specs=[pl.BlockSpec((t, dh), lambda b, h, *_: (b, h)), st, st],
        scratch_shapes=[
            pltpu.VMEM((t, dh), F32), pltpu.VMEM((t, dh), BF16), pltpu.VMEM((t, dh), F32),
            pltpu.VMEM((dh, dh), F32), pltpu.VMEM((dh, dh), F32),
        ],
    )
    st_shape = jax.ShapeDtypeStruct((n_batch, n_heads, dh, dh), F32)
    return pl.pallas_call(
        kern,
        grid_spec=grid_spec,
        out_shape=[jax.ShapeDtypeStruct((n_batch * t, rw), BF16), st_shape, st_shape],
        compiler_params=_params("parallel", "parallel"),
        name="retention",
    )(lgf, lgb, u, u, u, u, cos, sin, s0f, s0b)


def _ctx_state_kernel(lgf_ref, lgb_ref, k_ref, v_ref, stf_ref, stb_ref, *, layer, n_heads):
    h = pl.program_id(1)
    t = k_ref.shape[0]
    lgf = lgf_ref[layer * n_heads + h]
    lgb = lgb_ref[layer * n_heads + h]
    pos = lax.broadcasted_iota(jnp.int32, (t, 1), 0).astype(F32)
    k = k_ref[...]
    vb = v_ref[...].astype(BF16)
    stf_ref[...] = _dot_tn((k * jnp.exp(lgf * (t - 1.0 - pos))).astype(BF16), vb)
    stb_ref[...] = _dot_tn((k * jnp.exp(lgb * pos)).astype(BF16), vb)


def ctx_states(lgf, lgb, kv, layer, n_batch, t, n_heads, dh):
    kern = functools.partial(_ctx_state_kernel, layer=layer, n_heads=n_heads)
    st = pl.BlockSpec((None, None, dh, dh), lambda b, h, *_: (b, h, 0, 0))
    grid_spec = pltpu.PrefetchScalarGridSpec(
        num_scalar_prefetch=2,
        grid=(n_batch, n_heads),
        in_specs=[
            pl.BlockSpec((t, dh), lambda b, h, *_: (b, h)),
            pl.BlockSpec((t, dh), lambda b, h, *_: (b, n_heads + h)),
        ],
        out_specs=[st, st],
    )
    st_shape = jax.ShapeDtypeStruct((n_batch, n_heads, dh, dh), F32)
    return pl.pallas_call(
        kern,
        grid_spec=grid_spec,
        out_shape=[st_shape, st_shape],
        compiler_params=_params("parallel", "parallel"),
        name="ctx_states",
    )(lgf, lgb, kv, kv)


def _row_copy(src_ref, src_row, dst_ref, dst_row, sem):
    return pltpu.make_async_copy(src_ref.at[pl.ds(src_row, 1), :], dst_ref.at[pl.ds(dst_row, 1), :], sem)


def _rows_done(like_ref, n_rows, sem):
    span = like_ref.at[pl.ds(0, n_rows), :]
    return pltpu.make_async_copy(span, span, sem)


def _dispatch_kernel(slot_ref, ztile_ref, h_ref, xs_ref, zero_ref, sem_ref, zsem_ref, *, n_rows, n_exp, bmx):
    i = pl.program_id(0)
    tt = h_ref.shape[0]

    n_tiles = xs_ref.shape[0] // bmx

    def zero_copy(tile):
        start = pl.multiple_of(tile * bmx, bmx)
        return pltpu.make_async_copy(zero_ref, xs_ref.at[pl.ds(start, bmx), :], zsem_ref)

    @pl.when(i == 0)
    def _():
        zero_ref[...] = jnp.zeros_like(zero_ref)
        n_used = ztile_ref[n_exp]

        def start(e, carry):
            @pl.when(ztile_ref[e] >= 0)
            def _():
                zero_copy(ztile_ref[e]).start()
            return carry

        def wait(e, carry):
            @pl.when(ztile_ref[e] >= 0)
            def _():
                zero_copy(ztile_ref[e]).wait()
            return carry

        def start_tail(j, carry):
            zero_copy(j).start()
            return carry

        def wait_tail(j, carry):
            zero_copy(j).wait()
            return carry

        lax.fori_loop(0, n_exp, start, 0)
        lax.fori_loop(n_used, n_tiles, start_tail, 0)
        lax.fori_loop(0, n_exp, wait, 0)
        lax.fori_loop(n_used, n_tiles, wait_tail, 0)

    def issue(g, carry):
        first = pl.multiple_of(g * V7X_SUBLANES, V7X_SUBLANES)
        for j in range(V7X_SUBLANES):
            for k in range(TOP_K):
                _row_copy(h_ref, first + j, xs_ref, slot_ref[k * n_rows + i * tt + first + j], sem_ref).start()
        return carry

    lax.fori_loop(0, tt // V7X_SUBLANES, issue, 0)
    _rows_done(xs_ref, TOP_K * tt, sem_ref).wait()


def moe_dispatch(slots, ztile, h, n_slots, bmx):
    rows, d = h.shape
    n_exp = ztile.shape[0] - 1
    tt = ROW_TILE
    kern = functools.partial(_dispatch_kernel, n_rows=rows, n_exp=n_exp, bmx=bmx)
    grid_spec = pltpu.PrefetchScalarGridSpec(
        num_scalar_prefetch=2,
        grid=(rows // tt,),
        in_specs=[pl.BlockSpec((tt, d), lambda i, *_: (i, 0))],
        out_specs=pl.BlockSpec(memory_space=pl.ANY),
        scratch_shapes=[pltpu.VMEM((bmx, d), h.dtype), pltpu.SemaphoreType.DMA(()), pltpu.SemaphoreType.DMA(())],
    )
    return pl.pallas_call(
        kern,
        grid_spec=grid_spec,
        out_shape=jax.ShapeDtypeStruct((n_slots, d), h.dtype),
        compiler_params=_params("arbitrary"),
        name="moe_dispatch",
    )(slots, ztile, h)


def _experts_kernel(texp_ref, nused_ref, x_ref, w1_ref, w3_ref, w2_ref, o_ref, w1b_ref, w3b_ref, w2b_ref):
    i = pl.program_id(0)
    valid = i < nused_ref[0]
    fresh = (i == 0) | (valid & (texp_ref[i] != texp_ref[jnp.maximum(i - 1, 0)]))

    @pl.when(fresh)
    def _():
        w1b_ref[...] = w1_ref[...].astype(BF16)
        w3b_ref[...] = w3_ref[...].astype(BF16)
        w2b_ref[...] = w2_ref[...].astype(BF16)

    @pl.when(valid)
    def _():
        x = _unpack_bf16(x_ref[...])
        a = _silu(_dot(x, w1b_ref[...])) * _dot(x, w3b_ref[...])
        o_ref[...] = _pack_halves(_dot(a.astype(BF16), w2b_ref[...]))

    @pl.when(jnp.logical_not(valid))
    def _():
        o_ref[...] = jnp.zeros_like(o_ref)


def moe_experts(tile_expert, n_used, xs, w1, w3, w2, layer, bmx):
    n_slots, dp = xs.shape
    d, hid = w1.shape[2], w1.shape[3]
    tile = lambda i, texp, nused: jnp.minimum(i, nused[0] - 1)
    rows = pl.BlockSpec((bmx, dp), lambda i, texp, nused: (tile(i, texp, nused), 0))
    out_rows = pl.BlockSpec((bmx, dp), lambda i, texp, nused: (i, 0))
    up = pl.BlockSpec((None, None, d, hid), lambda i, texp, nused: (layer, texp[tile(i, texp, nused)], 0, 0))
    down = pl.BlockSpec((None, None, hid, d), lambda i, texp, nused: (layer, texp[tile(i, texp, nused)], 0, 0))
    grid_spec = pltpu.PrefetchScalarGridSpec(
        num_scalar_prefetch=2,
        grid=(n_slots // bmx,),
        in_specs=[rows, up, up, down],
        out_specs=out_rows,
        scratch_shapes=[pltpu.VMEM((d, hid), BF16), pltpu.VMEM((d, hid), BF16), pltpu.VMEM((hid, d), BF16)],
    )
    return pl.pallas_call(
        _experts_kernel,
        grid_spec=grid_spec,
        out_shape=jax.ShapeDtypeStruct((n_slots, dp), xs.dtype),
        compiler_params=_params("arbitrary"),
        name="moe_experts",
    )(tile_expert, n_used, xs, w1, w3, w2)


def _combine_kernel(slot_ref, w_ref, ys_ref, o_ref, buf_ref, sem_ref, *, n_rows):
    i = pl.program_id(0)
    n = pl.num_programs(0)
    tt = o_ref.shape[0]

    def issue(tile, buf):
        def body(g, carry):
            first = pl.multiple_of(g * V7X_SUBLANES, V7X_SUBLANES)
            for j in range(V7X_SUBLANES):
                for k in range(TOP_K):
                    _row_copy(ys_ref, slot_ref[k * n_rows + tile * tt + first + j], buf_ref.at[buf],
                              k * tt + first + j, sem_ref.at[buf]).start()
            return carry

        lax.fori_loop(0, tt // V7X_SUBLANES, body, 0)

    @pl.when(i == 0)
    def _():
        issue(0, 0)

    @pl.when(i + 1 < n)
    def _():
        issue(i + 1, (i + 1) % 2)

    buf = i % 2
    _rows_done(ys_ref, TOP_K * tt, sem_ref.at[buf]).wait()
    w = w_ref[...]
    half = ys_ref.shape[1]
    rc, lc = 32, 2 * V7X_LANES
    for r0 in range(0, tt, rc):
        wr = w[r0:r0 + rc]
        for c0 in range(0, half, lc):
            acc_lo = jnp.zeros((rc, lc), F32)
            acc_hi = jnp.zeros((rc, lc), F32)
            for k in range(TOP_K):
                lo, hi = _unpack_halves(buf_ref[buf, k * tt + r0:k * tt + r0 + rc, c0:c0 + lc])
                acc_lo = acc_lo + lo * wr[:, k:k + 1]
                acc_hi = acc_hi + hi * wr[:, k:k + 1]
            o_ref[r0:r0 + rc, c0:c0 + lc] = acc_lo
            o_ref[r0:r0 + rc, half + c0:half + c0 + lc] = acc_hi


def moe_combine(slots, wts, ys, rows):
    dp = ys.shape[1]
    tt = COMBINE_TT
    kern = functools.partial(_combine_kernel, n_rows=rows)
    grid_spec = pltpu.PrefetchScalarGridSpec(
        num_scalar_prefetch=1,
        grid=(rows // tt,),
        in_specs=[pl.BlockSpec((tt, TOP_K), lambda i, *_: (i, 0)), pl.BlockSpec(memory_space=pl.ANY)],
        out_specs=pl.BlockSpec((tt, 2 * dp), lambda i, *_: (i, 0)),
        scratch_shapes=[pltpu.VMEM((2, TOP_K * tt, dp), ys.dtype), pltpu.SemaphoreType.DMA((2,))],
    )
    return pl.pallas_call(
        kern,
        grid_spec=grid_spec,
        out_shape=jax.ShapeDtypeStruct((rows, 2 * dp), F32),
        compiler_params=_params("arbitrary"),
        name="moe_combine",
    )(slots, wts, ys)


def routing_tables(eidx, pos, counts, bmx, n_tiles):
    n_exp = counts.shape[0]
    padded = (counts + bmx - 1) // bmx * bmx
    ends = jnp.cumsum(padded)
    starts = ends - padded
    experts = jnp.arange(n_exp, dtype=jnp.int32)[:, None, None]
    slots = (jnp.sum(jnp.where(eidx[None] == experts, starts[:, None, None], 0), axis=0) + pos).reshape(-1)
    tile_ends = ends // bmx
    n_used = tile_ends[-1:]
    tiles = jnp.arange(n_tiles, dtype=jnp.int32)
    tile_expert = jnp.sum((tile_ends[None, :] <= tiles[:, None]).astype(jnp.int32), axis=1)
    tile_expert = jnp.minimum(tile_expert, n_exp - 1).astype(jnp.int32)
    ztile = jnp.concatenate([jnp.where(counts > 0, tile_ends - 1, -1), n_used]).astype(jnp.int32)
    return slots.astype(jnp.int32), ztile, tile_expert, n_used.astype(jnp.int32)


def _fused_experts_kernel(texp_ref, nused_ref, tab_ref, h_ref, w1_ref, w3_ref, w2_ref, ys_ref,
                          w1b_ref, w3b_ref, w2b_ref, xbuf0, xbuf1, ybuf0, ybuf1, gsem, ssem, zsem,
                          *, n_pairs, bmx):
    i = pl.program_id(0)
    n_steps = pl.num_programs(0)
    n_used = nused_ref[0]
    valid = i < n_used
    xbufs, ybufs = (xbuf0, xbuf1), (ybuf0, ybuf1)

    def row_groups(tile, copy_row):
        def body(g, carry):
            first = pl.multiple_of(g * COPY_GROUP, COPY_GROUP)
            for j in range(COPY_GROUP):
                copy_row(first + j, tab_ref[(tile + 1) * bmx + first + j])
            return carry

        lax.fori_loop(0, bmx // COPY_GROUP, body, 0)

    def gather(tile, buf):
        def copy_row(r, entry):
            src = h_ref.at[pl.ds(entry >> ROW_BITS, 1), :]
            pltpu.make_async_copy(src, xbufs[buf].at[pl.ds(r, 1), :], gsem.at[buf]).start()

        row_groups(tile, copy_row)

    def scatter(tile, buf):
        def copy_row(r, entry):
            dst = ys_ref.at[pl.ds(entry & ((1 << ROW_BITS) - 1), 1), :]
            pltpu.make_async_copy(ybufs[buf].at[pl.ds(r, 1), :], dst, ssem.at[buf]).start()

        row_groups(tile, copy_row)

    def tile_done(sem):
        return _rows_done(ys_ref, bmx, sem)

    @pl.when(i == 0)
    def _():
        ybuf1[...] = jnp.zeros_like(ybuf1)
        n_fill = (ys_ref.shape[0] - n_pairs) // bmx

        def fill(j):
            start = pl.multiple_of(n_pairs + j * bmx, bmx)
            return pltpu.make_async_copy(ybuf1, ys_ref.at[pl.ds(start, bmx), :], zsem)

        def fill_start(j, carry):
            fill(j).start()
            return carry

        def fill_wait(j, carry):
            fill(j).wait()
            return carry

        lax.fori_loop(0, n_fill, fill_start, 0)
        lax.fori_loop(0, n_fill, fill_wait, 0)
        gather(0, 0)

    fresh = (i == 0) | (valid & (texp_ref[i] != texp_ref[jnp.maximum(i - 1, 0)]))

    @pl.when(fresh)
    def _():
        w1b_ref[...] = w1_ref[...].astype(BF16)
        w3b_ref[...] = w3_ref[...].astype(BF16)
        w2b_ref[...] = w2_ref[...].astype(BF16)

    for p in range(2):
        q = 1 - p
        mine = i % 2 == p

        @pl.when(valid & mine)
        def _():
            tile_done(gsem.at[p]).wait()
            gather(i + 1, q)
            scatter(i - 1, q)
            x = _unpack_bf16(xbufs[p][...])
            a = _silu(_dot(x, w1b_ref[...])) * _dot(x, w3b_ref[...])
            ybufs[p][...] = _pack_halves(_dot(a.astype(BF16), w2b_ref[...]))
            tile_done(ssem.at[q]).wait()

        @pl.when((i == n_used) & mine)
        def _():
            tile_done(gsem.at[p]).wait()
            scatter(i - 1, q)
            tile_done(ssem.at[q]).wait()

        @pl.when((i == n_steps - 1) & valid & mine)
        def _():
            tile_done(gsem.at[q]).wait()
            scatter(i, p)
            tile_done(ssem.at[p]).wait()


def moe_experts_fused(tile_expert, n_used, table, h, w1, w3, w2, layer, n_tiles, bmx):
    n_tok, dp = h.shape
    d, hid = w1.shape[2], w1.shape[3]
    n_pairs = TOP_K * n_tok
    n_rows = (n_tiles + 1) * bmx
    assert n_rows <= 1 << ROW_BITS and n_tok <= 1 << (31 - ROW_BITS)
    expert = lambda i, texp, nused, *_: (layer, texp[jnp.minimum(i, nused[0] - 1)], 0, 0)
    up = pl.BlockSpec((None, None, d, hid), expert)
    down = pl.BlockSpec((None, None, hid, d), expert)
    kern = functools.partial(_fused_experts_kernel, n_pairs=n_pairs, bmx=bmx)
    grid_spec = pltpu.PrefetchScalarGridSpec(
        num_scalar_prefetch=3,
        grid=(n_tiles,),
        in_specs=[pl.BlockSpec(memory_space=pl.ANY), up, up, down],
        out_specs=pl.BlockSpec(memory_space=pl.ANY),
        scratch_shapes=[
            pltpu.VMEM((d, hid), BF16), pltpu.VMEM((d, hid), BF16), pltpu.VMEM((hid, d), BF16),
            pltpu.VMEM((bmx, dp), h.dtype), pltpu.VMEM((bmx, dp), h.dtype),
            pltpu.VMEM((bmx, dp), h.dtype), pltpu.VMEM((bmx, dp), h.dtype),
            pltpu.SemaphoreType.DMA((2,)), pltpu.SemaphoreType.DMA((2,)), pltpu.SemaphoreType.DMA(()),
        ],
    )
    return pl.pallas_call(
        kern,
        grid_spec=grid_spec,
        out_shape=jax.ShapeDtypeStruct((n_rows, dp), h.dtype),
        compiler_params=_params("arbitrary"),
        name="moe_experts_fused",
    )(tile_expert, n_used, table, h, w1, w3, w2)


def _dense_combine_kernel(w_ref, *refs):
    y_refs, o_ref = refs[:TOP_K], refs[TOP_K]
    tt = o_ref.shape[0]
    half = y_refs[0].shape[1]
    w = w_ref[...]
    rc, lc = 32, 2 * V7X_LANES
    for r0 in range(0, tt, rc):
        wr = w[r0:r0 + rc]
        for c0 in range(0, half, lc):
            acc_lo = jnp.zeros((rc, lc), F32)
            acc_hi = jnp.zeros((rc, lc), F32)
            for k in range(TOP_K):
                lo, hi = _unpack_halves(y_refs[k][r0:r0 + rc, c0:c0 + lc])
                acc_lo = acc_lo + lo * wr[:, k:k + 1]
                acc_hi = acc_hi + hi * wr[:, k:k + 1]
            o_ref[r0:r0 + rc, c0:c0 + lc] = acc_lo
            o_ref[r0:r0 + rc, half + c0:half + c0 + lc] = acc_hi


def moe_dense_combine(wts, ys, rows):
    dp = ys.shape[1]
    tt = COMBINE_TT
    per_k = rows // tt
    y_specs = [pl.BlockSpec((tt, dp), functools.partial(lambda i, k: (k * per_k + i, 0), k=k)) for k in range(TOP_K)]
    return pl.pallas_call(
        _dense_combine_kernel,
        grid=(rows // tt,),
        in_specs=[pl.BlockSpec((tt, TOP_K), lambda i: (i, 0))] + y_specs,
        out_specs=pl.BlockSpec((tt, 2 * dp), lambda i: (i, 0)),
        out_shape=jax.ShapeDtypeStruct((rows, 2 * dp), F32),
        compiler_params=_params("parallel"),
        name="moe_dense_combine",
    )(wts, *([ys] * TOP_K))


def fused_routing_tables(eidx, pos, counts, bmx, n_tiles):
    n_exp = counts.shape[0]
    n_tok = eidx.shape[1]
    n_pairs = TOP_K * n_tok
    n_slots = n_tiles * bmx
    n_pad = n_slots - n_pairs
    padded = (counts + bmx - 1) // bmx * bmx
    ends = jnp.cumsum(padded)
    starts = ends - padded
    experts = jnp.arange(n_exp, dtype=jnp.int32)
    slots = jnp.sum(jnp.where(eidx[None] == experts[:, None, None], starts[:, None, None], 0), axis=0) + pos
    pad_counts = jnp.concatenate([padded - counts, n_slots - ends[-1:]])
    pad_first = jnp.concatenate([starts + counts, ends[-1:]])
    pad_ends = jnp.cumsum(pad_counts)
    j = jnp.arange(n_pad, dtype=jnp.int32)
    owner = jnp.sum((pad_ends[None, :] <= j[:, None]).astype(jnp.int32), axis=1)
    owners = jnp.arange(n_exp + 1, dtype=jnp.int32)
    base = jnp.sum(jnp.where(owner[:, None] == owners[None, :], (pad_first - pad_ends + pad_counts)[None, :], 0), axis=1)
    pad_slot = base + j
    keys = jnp.concatenate([slots.reshape(-1), pad_slot]).astype(jnp.int32)
    rows = jnp.arange(n_slots, dtype=jnp.int32)
    toks = jnp.concatenate([jnp.tile(jnp.arange(n_tok, dtype=jnp.int32), TOP_K), jnp.zeros((n_pad,), jnp.int32)])
    _, rows, toks = lax.sort((keys, rows, toks), num_keys=1)
    spare = n_slots + jnp.arange(bmx, dtype=jnp.int32)
    rows = jnp.concatenate([spare, rows, spare])
    toks = jnp.concatenate([jnp.zeros_like(spare), toks, jnp.zeros_like(spare)])
    table = (toks << ROW_BITS) | rows
    tile_ends = ends // bmx
    n_used = tile_ends[-1:].astype(jnp.int32)
    tiles = jnp.arange(n_tiles, dtype=jnp.int32)
    tile_expert = jnp.sum((tile_ends[None, :] <= tiles[:, None]).astype(jnp.int32), axis=1)
    tile_expert = jnp.minimum(tile_expert, n_exp - 1).astype(jnp.int32)
    return table, tile_expert, n_used


def _glu_mm_kernel(h_ref, w1_ref, w3_ref, o_ref, w1b_ref, w3b_ref):
    @pl.when(pl.program_id(1) == 0)
    def _():
        w1b_ref[...] = w1_ref[...].astype(BF16)
        w3b_ref[...] = w3_ref[...].astype(BF16)

    h = _unpack_bf16(h_ref[...])
    o_ref[...] = (_silu(_dot(h, w1b_ref[...])) * _dot(h, w3b_ref[...])).astype(o_ref.dtype)


def glu_matmul(h, w1, w3, layer, rows, bm):
    k = w1.shape[1]
    n = w1.shape[2]
    bn = HID_CHUNK
    return pl.pallas_call(
        _glu_mm_kernel,
        grid=(n // bn, rows // bm),
        in_specs=[
            pl.BlockSpec((bm, h.shape[1]), lambda j, i: (i, 0)),
            pl.BlockSpec((None, k, bn), lambda j, i: (layer, 0, j)),
            pl.BlockSpec((None, k, bn), lambda j, i: (layer, 0, j)),
        ],
        out_specs=pl.BlockSpec((bm, bn), lambda j, i: (i, j)),
        out_shape=jax.ShapeDtypeStruct((rows, n), BF16),
        scratch_shapes=[pltpu.VMEM((k, bn), BF16), pltpu.VMEM((k, bn), BF16)],
        compiler_params=_params("arbitrary", "arbitrary"),
        name="glu_matmul",
    )(h, w1, w3)


def _rope_tables(seq, dh):
    rows = seq // GRID_W
    row = jnp.repeat(jnp.arange(rows, dtype=F32), GRID_W)
    col = jnp.tile(jnp.arange(GRID_W, dtype=F32), rows)
    quarter = dh // 4
    inv_freq = ROPE_BASE ** (-jnp.arange(quarter, dtype=F32) / quarter)
    ang_r = row[:, None] * inv_freq[None, :]
    ang_c = col[:, None] * inv_freq[None, :]
    cos = jnp.concatenate([jnp.cos(ang_r), jnp.cos(ang_r), jnp.cos(ang_c), jnp.cos(ang_c)], axis=-1)
    sin = jnp.concatenate([-jnp.sin(ang_r), jnp.sin(ang_r), -jnp.sin(ang_c), jnp.sin(ang_c)], axis=-1)
    return cos, sin


def kernel(x, c, ctx, c_ctx, mod_w, mod_b, norm1_g, norm2_g, w_in, conv_w, conv_b, conv_ln_g, conv_ln_b,
           w_conv_out, ret_log_gamma_fwd, ret_log_gamma_bwd, w_ret_out, w_merge_out, router_w, router_bias,
           exp_w1, exp_w3, exp_w2, shared_w1, shared_w3, shared_w2, final_g):
    b, s, d = x.shape
    n_ctx = ctx.shape[1]
    depth = mod_w.shape[0]
    cc = conv_w.shape[2]
    n_heads = ret_log_gamma_fwd.shape[1]
    rw = w_ret_out.shape[1]
    dh = rw // n_heads
    n_exp = router_w.shape[2]
    n_lat, n_cx = b * s, b * n_ctx
    n_all = n_lat + n_cx
    bm = min(MM_BM, s, n_cx)
    assert b + 1 <= 8 and s % ROW_TILE == 0 and n_ctx % ROW_TILE == 0 and dh == 2 * V7X_LANES
    assert s % bm == 0 and n_cx % bm == 0 and s % MOE_TM == 0 and n_cx % MOE_TM == 0
    q_off, gate_off = 2 * cc, 2 * cc + 4 * rw

    xa = jnp.concatenate([x.reshape(n_lat, d), ctx.reshape(n_cx, d)], axis=0)
    c8 = jnp.zeros((8, d), F32).at[:b].set(c).at[b].set(c_ctx)
    mod_b3 = mod_b.reshape(depth, 1, -1)
    n1g, n2g = norm1_g.reshape(depth, 1, d), norm2_g.reshape(depth, 1, d)
    conv_b3, ln_g3, ln_b3 = (a.reshape(depth, 1, cc) for a in (conv_b, conv_ln_g, conv_ln_b))
    lgf, lgb = ret_log_gamma_fwd.reshape(-1), ret_log_gamma_bwd.reshape(-1)
    rw_t = jnp.swapaxes(router_w, 1, 2)
    rw_hi = rw_t.astype(BF16)
    rw_lo = (rw_t - rw_hi.astype(F32)).astype(BF16)
    rw_cat = jnp.concatenate([rw_hi, rw_lo], axis=1)
    bias3 = router_bias.reshape(depth, n_exp, 1)
    cos, sin = _rope_tables(s, dh)
    zero_st = jnp.zeros((b, n_heads, dh, dh), F32)

    for layer in range(depth):
        last = layer == depth - 1
        rows = n_lat if last else n_all
        seq_args = dict(n_lat_rows=n_lat, seq=s, n_batch=b)
        mod3 = modulation(c8, mod_w, mod_b3, layer).reshape(8 * N_MOD, 1, d)
        h1 = norm_mod(xa, n1g, mod3, layer, 0, 1, n_all, **seq_args)
        if last:
            u = matmul(h1, w_in, layer, n_lat, 0, 0, w_in.shape[2], bm)
            kv_c = matmul(h1, w_in, layer, n_cx, n_lat, q_off + rw, 2 * rw, bm)
            st_f, st_b = ctx_states(lgf, lgb, kv_c, layer, b, n_ctx, n_heads, dh)
        else:
            u = matmul(h1, w_in, layer, n_all, 0, 0, w_in.shape[2], bm)
            yr_c, st_f, st_b = retention(lgf, lgb, u, cos, sin, zero_st, zero_st, layer, b, n_ctx, n_lat, q_off,
                                         n_heads, dh, rope=False, zero_init=True)
        yc = conv_branch(u, conv_w, conv_b3, ln_g3, ln_b3, layer, rows, n_lat, s, n_ctx)
        yr, _, _ = retention(lgf, lgb, u, cos, sin, st_f, st_b, layer, b, s, 0, q_off, n_heads, dh,
                             rope=True, zero_init=False)
        if not last:
            yr = jnp.concatenate([yr, yr_c], axis=0)
        mixed = merge_branches(yc, yr, u, w_conv_out, w_ret_out, layer, rows, gate_off, bm)
        xa = resid_matmul(mixed, w_merge_out, xa, mod3, layer, 2, rows, bm=bm, **seq_args)
        h2, eidx, pos, wts, cnt = norm_router(xa, n2g, mod3, rw_cat, rw_hi, bias3, layer, 3, 4, rows, **seq_args)
        n_tiles = TOP_K * rows // MOE_BMX + n_exp
        slots, ztile, tile_expert, n_used = routing_tables(eidx, pos, cnt[:, 0].astype(jnp.int32), MOE_BMX, n_tiles)
        xs = moe_dispatch(slots, ztile, h2, n_tiles * MOE_BMX, MOE_BMX)
        ys = moe_experts(tile_expert, n_used, xs, exp_w1, exp_w3, exp_w2, layer, MOE_BMX)
        routed = moe_combine(slots, wts.T, ys, rows)
        act = glu_matmul(h2, shared_w1, shared_w3, layer, rows, MOE_TM)
        xa = resid_matmul(act, shared_w2, xa, mod3, layer, 5, rows, bm=bm, extra=routed, **seq_args)
    out = final_norm(xa, final_g.reshape(1, d), n_lat)
    return out.reshape(b, s, d)
```
